```python
import math
import jax, jax.numpy as jnp
from jax import lax
import numpy as np

D_MODEL = 1024
BATCH = 2
SEQ = 16384
DEPTH = 1
DEC_BATCH = 8
DEC_SEQ = 4096
PAST_LEN = 128

GRID_W = 64
NA_HEADS = 8
NA_HEAD_DIM = 64
NA_WIDTH = NA_HEADS * NA_HEAD_DIM
WIN_ROWS = 8
WIN_COLS = 16
SGU_GROUPS = 8
SGU_GROUP_DIM = 64
SGU_WIDTH = SGU_GROUPS * SGU_GROUP_DIM
CHUNK = 128
D_FF = 4 * D_MODEL
IN_WIDTH = 3 * NA_WIDTH + 2 * SGU_WIDTH + 2 * D_MODEL
N_ADA = 6
ALPHA = (2.0 * DEPTH) ** 0.25
BETA = (8.0 * DEPTH) ** -0.25
LN_EPS = 1e-5

kernel_name = "hybrid_natten_gmlp_deepnorm_adaln_encoder"


def layer_norm(x, g, b):
    xf = x.astype(jnp.float32)
    mu = jnp.mean(xf, axis=-1, keepdims=True)
    var = jnp.mean(jnp.square(xf - mu), axis=-1, keepdims=True)
    y = (xf - mu) * lax.rsqrt(var + LN_EPS) * g.astype(jnp.float32) + b.astype(jnp.float32)
    return y.astype(x.dtype)


def neighborhood_attention(q, k, v, rpb):
    B, T, H, dh = q.shape
    rows = T // GRID_W
    wr = min(WIN_ROWS, rows)
    qg = q.reshape(B, rows, GRID_W, H, dh) * (dh ** -0.5)
    kg = k.reshape(B, rows, GRID_W, H, dh)
    vg = v.reshape(B, rows, GRID_W, H, dh)
    col = jnp.arange(GRID_W)
    col_start = jnp.clip(col - WIN_COLS // 2, 0, GRID_W - WIN_COLS)
    col_idx = col_start[:, None] + jnp.arange(WIN_COLS)[None, :]
    dcol = col_idx - col[:, None] + (WIN_COLS - 1)
    rpb_c = jnp.transpose(rpb[:, :, dcol], (0, 2, 1, 3)).astype(jnp.float32)

    def row_fn(r):
        rs = jnp.clip(r - WIN_ROWS // 2, 0, rows - wr)
        k_rows = lax.dynamic_slice_in_dim(kg, rs, wr, axis=1)
        v_rows = lax.dynamic_slice_in_dim(vg, rs, wr, axis=1)
        k_win = k_rows[:, :, col_idx]
        v_win = v_rows[:, :, col_idx]
        q_r = lax.dynamic_index_in_dim(qg, r, axis=1, keepdims=False)
        s = jnp.einsum('bqhd,biqjhd->bhqij', q_r, k_win).astype(jnp.float32)
        drow = rs + jnp.arange(wr) - r + (WIN_ROWS - 1)
        s = s + rpb_c[:, :, drow][None]
        p = jax.nn.softmax(s.reshape(B, H, GRID_W, wr * WIN_COLS), axis=-1)
        p = p.reshape(B, H, GRID_W, wr, WIN_COLS).astype(v.dtype)
        return jnp.einsum('bhqij,biqjhd->bqhd', p, v_win)

    out = lax.map(row_fn, jnp.arange(rows))
    return jnp.transpose(out, (1, 0, 2, 3, 4)).reshape(B, T, H * dh)


def spatial_gating(u, vs, ln_g, ln_b, w_s, b_s):
    B, T, _ = vs.shape
    vs = layer_norm(vs, ln_g, ln_b)
    vc = vs.reshape(B, T // CHUNK, CHUNK, SGU_GROUPS, SGU_GROUP_DIM)
    s = jnp.einsum('gpq,bnqgc->bnpgc', w_s, vc) + jnp.transpose(b_s)[None, None, :, :, None]
    return u * s.reshape(B, T, SGU_WIDTH)


def encoder_layer(x, c, w_ada, b_ada, w_in, rpb, sgu_ln_g, sgu_ln_b, w_s, b_s,
                  w_attn_up, w_sgu_up, w_o, ln1_g, ln1_b, w_ff1, b_ff1, w_ff2, b_ff2,
                  ln2_g, ln2_b):
    B, T, _ = x.shape
    ada = jax.nn.silu(c) @ w_ada + b_ada
    shift1, scale1, gate1, shift2, scale2, gate2 = [a[:, None, :] for a in jnp.split(ada, N_ADA, axis=-1)]

    h = x * (1.0 + scale1) + shift1
    z = h @ w_in
    splits = np.cumsum([NA_WIDTH, NA_WIDTH, NA_WIDTH, SGU_WIDTH, SGU_WIDTH, D_MODEL]).tolist()
    q, k, v, u, vs, ga, gb = jnp.split(z, splits, axis=-1)
    attn = neighborhood_attention(q.reshape(B, T, NA_HEADS, NA_HEAD_DIM),
                                  k.reshape(B, T, NA_HEADS, NA_HEAD_DIM),
                                  v.reshape(B, T, NA_HEADS, NA_HEAD_DIM), rpb)
    sgu = spatial_gating(jax.nn.gelu(u), jax.nn.gelu(vs), sgu_ln_g, sgu_ln_b, w_s, b_s)
    merged = jax.nn.sigmoid(ga) * (attn @ w_attn_up) + jax.nn.sigmoid(gb) * (sgu @ w_sgu_up)
    mix = merged @ w_o
    x = layer_norm(ALPHA * x + gate1 * mix, ln1_g, ln1_b)

    h = x * (1.0 + scale2) + shift2
    f = jnp.square(jax.nn.relu(h @ w_ff1 + b_ff1)) @ w_ff2 + b_ff2
    x = layer_norm(ALPHA * x + gate2 * f, ln2_g, ln2_b)
    return x


def run_trunk(x, c, w_ada, b_ada, w_in, rpb, sgu_ln_g, sgu_ln_b, w_s, b_s,
              w_attn_up, w_sgu_up, w_o, ln1_g, ln1_b, w_ff1, b_ff1, w_ff2, b_ff2,
              ln2_g, ln2_b):
    for l in range(DEPTH):
        x = encoder_layer(x, c, w_ada[l], b_ada[l], w_in[l], rpb[l], sgu_ln_g[l], sgu_ln_b[l],
                          w_s[l], b_s[l], w_attn_up[l], w_sgu_up[l], w_o[l], ln1_g[l], ln1_b[l],
                          w_ff1[l], b_ff1[l], w_ff2[l], b_ff2[l], ln2_g[l], ln2_b[l])
    return x


def setup_inputs(seed: int = 0) -> dict:
    key = jax.random.key(seed)
    ks = jax.random.split(key, 24)
    f32 = jnp.float32
    L, D = DEPTH, D_MODEL

    def nrm(k, shape, std):
        return jax.random.normal(k, shape, f32) * std

    return {
        "x_prompt": nrm(ks[0], (BATCH, SEQ, D), 1.0),
        "x_sample": nrm(ks[1], (DEC_BATCH, DEC_SEQ, D), 1.0),
        "c_prompt": nrm(ks[2], (BATCH, D), 1.0),
        "c_sample": nrm(ks[3], (DEC_BATCH, D), 1.0),
        "w_ada": nrm(ks[4], (L, D, N_ADA * D), 0.5 * D ** -0.5),
        "b_ada": nrm(ks[5], (L, N_ADA * D), 0.02),
        "w_in": nrm(ks[6], (L, D, IN_WIDTH), D ** -0.5),
        "rpb": nrm(ks[7], (L, NA_HEADS, 2 * WIN_ROWS - 1, 2 * WIN_COLS - 1), 0.1),
        "sgu_ln_g": 1.0 + nrm(ks[8], (L, SGU_WIDTH), 0.02),
        "sgu_ln_b": nrm(ks[9], (L, SGU_WIDTH), 0.02),
        "w_s": nrm(ks[10], (L, SGU_GROUPS, CHUNK, CHUNK), 0.5 * CHUNK ** -0.5),
        "b_s": 1.0 + nrm(ks[11], (L, SGU_GROUPS, CHUNK), 0.02),
        "w_attn_up": nrm(ks[12], (L, NA_WIDTH, D), NA_WIDTH ** -0.5),
        "w_sgu_up": nrm(ks[13], (L, SGU_WIDTH, D), SGU_WIDTH ** -0.5),
        "w_o": nrm(ks[14], (L, D, D), BETA * D ** -0.5),
        "ln1_g": 1.0 + nrm(ks[15], (L, D), 0.02),
        "ln1_b": nrm(ks[16], (L, D), 0.02),
        "w_ff1": nrm(ks[17], (L, D, D_FF), D ** -0.5),
        "b_ff1": nrm(ks[18], (L, D_FF), 0.02),
        "w_ff2": nrm(ks[19], (L, D_FF, D), BETA * D_FF ** -0.5),
        "b_ff2": nrm(ks[20], (L, D), 0.02),
        "ln2_g": 1.0 + nrm(ks[21], (L, D), 0.02),
        "ln2_b": nrm(ks[22], (L, D), 0.02),
    }


def reference(x_prompt, x_sample, c_prompt, c_sample, w_ada, b_ada, w_in, rpb, sgu_ln_g, sgu_ln_b,
              w_s, b_s, w_attn_up, w_sgu_up, w_o, ln1_g, ln1_b, w_ff1, b_ff1, w_ff2, b_ff2,
              ln2_g, ln2_b):
    y_prompt = run_trunk(x_prompt, c_prompt, w_ada, b_ada, w_in, rpb, sgu_ln_g, sgu_ln_b, w_s, b_s,
                         w_attn_up, w_sgu_up, w_o, ln1_g, ln1_b, w_ff1, b_ff1, w_ff2, b_ff2,
                         ln2_g, ln2_b)
    y_sample = run_trunk(x_sample, c_sample, w_ada, b_ada, w_in, rpb, sgu_ln_g, sgu_ln_b, w_s, b_s,
                         w_attn_up, w_sgu_up, w_o, ln1_g, ln1_b, w_ff1, b_ff1, w_ff2, b_ff2,
                         ln2_g, ln2_b)
    return (y_prompt, y_sample)
```

```python
import functools

import numpy as np
import jax
import jax.numpy as jnp
from jax import lax
from jax.experimental import pallas as pl
from jax.experimental.pallas import tpu as pltpu

GRID_W = 64
NA_HEADS = 8
NA_HEAD_DIM = 64
NA_WIDTH = NA_HEADS * NA_HEAD_DIM
WIN_ROWS = 8
WIN_COLS = 16
SGU_GROUPS = 8
SGU_GROUP_DIM = 64
SGU_WIDTH = SGU_GROUPS * SGU_GROUP_DIM
CHUNK = 128
N_ADA = 6
LN_EPS = 1e-5
MASK_VALUE = -1e30

MXU_DIM = 256
LANES = 128
VMEM_LIMIT_BYTES = 48 * 1024 * 1024

TOKEN_TILE = 512
ATTN_ROWS = 8
FF_CHUNK = 1024

BF16 = jnp.bfloat16
F32 = jnp.float32
_SQRT_2_OVER_PI = np.float32(np.sqrt(2.0 / np.pi))


def _gelu_tanh(x):
    return x * (0.5 * (1.0 + jnp.tanh(_SQRT_2_OVER_PI * (x + 0.044715 * (x * x * x)))))


def _layer_norm(x, g, b):
    mu = jnp.mean(x, axis=-1, keepdims=True)
    xc = x - mu
    var = jnp.mean(xc * xc, axis=-1, keepdims=True)
    return xc * lax.rsqrt(var + LN_EPS) * g + b


def _resident(shape):
    zeros = (0,) * len(shape)
    return pl.BlockSpec(shape, lambda *_: zeros, pipeline_mode=pl.Buffered(1))


def _ada_kernel(c_ref, w_ref, b_ref, o_ref):
    c = c_ref[...]
    s = c * jax.nn.sigmoid(c)
    o_ref[...] = jnp.dot(s, w_ref[...], preferred_element_type=F32,
                         precision=lax.Precision.HIGHEST) + b_ref[...]


def _ada(c, w_ada, b_ada):
    nb, d = c.shape
    n = w_ada.shape[1]
    return pl.pallas_call(
        _ada_kernel,
        grid=(n // d,),
        in_specs=[pl.BlockSpec((nb, d), lambda j: (0, 0)),
                  pl.BlockSpec((d, d), lambda j: (0, j)),
                  pl.BlockSpec((1, d), lambda j: (0, j))],
        out_specs=pl.BlockSpec((nb, d), lambda j: (0, j)),
        out_shape=jax.ShapeDtypeStruct((nb, n), F32),
        name="ada",
        compiler_params=pltpu.CompilerParams(vmem_limit_bytes=VMEM_LIMIT_BYTES),
    )(c, w_ada, b_ada.reshape(1, n))


def _inproj_kernel(x_ref, ada_ref, w_ref, lng_ref, lnb_ref,
                   q_ref, k_ref, v_ref, u_ref, vn_ref, sa_ref, sb_ref):
    shift = ada_ref[0:1, :]
    scale = ada_ref[1:2, :]
    h = (x_ref[...] * (1.0 + scale) + shift).astype(BF16)

    def seg(lo, width):
        return jnp.dot(h, w_ref[:, lo:lo + width], preferred_element_type=F32)

    q_ref[...] = (seg(0, NA_WIDTH) * (NA_HEAD_DIM ** -0.5)).astype(BF16)
    k_ref[...] = seg(NA_WIDTH, NA_WIDTH).astype(BF16)
    v_ref[...] = seg(2 * NA_WIDTH, NA_WIDTH).astype(BF16)
    lo = 3 * NA_WIDTH
    u_ref[...] = _gelu_tanh(seg(lo, SGU_WIDTH)).astype(BF16)
    lo += SGU_WIDTH
    vs = _gelu_tanh(seg(lo, SGU_WIDTH))
    vn_ref[...] = _layer_norm(vs, lng_ref[...], lnb_ref[...]).astype(BF16)
    lo += SGU_WIDTH
    d = sa_ref.shape[-1]
    sa_ref[...] = jax.nn.sigmoid(seg(lo, d)).astype(BF16)
    sb_ref[...] = jax.nn.sigmoid(seg(lo + d, d)).astype(BF16)


def _inproj(x, ada, w_in, ln_g, ln_b):
    b, t, d = x.shape
    tm = TOKEN_TILE
    tok = lambda w: pl.BlockSpec((None, tm, w), lambda bi, i: (bi, i, 0))
    widths = (NA_WIDTH, NA_WIDTH, NA_WIDTH, SGU_WIDTH, SGU_WIDTH, d, d)
    return pl.pallas_call(
        _inproj_kernel,
        grid=(b, t // tm),
        in_specs=[tok(d),
                  pl.BlockSpec((None, N_ADA, d), lambda bi, i: (bi, 0, 0)),
                  _resident(w_in.shape),
                  _resident((1, SGU_WIDTH)),
                  _resident((1, SGU_WIDTH))],
        out_specs=[tok(w) for w in widths],
        out_shape=[jax.ShapeDtypeStruct((b, t, w), BF16) for w in widths],
        name="inproj",
        compiler_params=pltpu.CompilerParams(
            dimension_semantics=("parallel", "parallel"),
            vmem_limit_bytes=VMEM_LIMIT_BYTES),
    )(x, ada, w_in, ln_g.reshape(1, -1), ln_b.reshape(1, -1))


def _attn_kernel(q_ref, kp_ref, kc_ref, kn_ref, vp_ref, vc_ref, vn_ref, bias_ref,
                 o_ref, kw_ref, vw_ref, *, rows):
    i = pl.program_id(1)
    tb = ATTN_ROWS * GRID_W
    win = WIN_ROWS * GRID_W
    kw_ref[0:tb, :] = kp_ref[...]
    kw_ref[tb:2 * tb, :] = kc_ref[...]
    kw_ref[2 * tb:3 * tb, :] = kn_ref[...]
    vw_ref[0:tb, :] = vp_ref[...]
    vw_ref[tb:2 * tb, :] = vc_ref[...]
    vw_ref[2 * tb:3 * tb, :] = vn_ref[...]

    heads_per_group = MXU_DIM // NA_HEAD_DIM
    lane_head = lax.broadcasted_iota(jnp.int32, (GRID_W, MXU_DIM), 1) // NA_HEAD_DIM
    head_masks = [lane_head == hh for hh in range(heads_per_group)]

    def row_body(rr, carry):
        r = i * ATTN_ROWS + rr
        rs = jnp.clip(r - WIN_ROWS // 2, 0, rows - WIN_ROWS)
        delta = r - rs
        koff = pl.multiple_of((rs - (i - 1) * ATTN_ROWS) * GRID_W, GRID_W)
        qoff = pl.multiple_of(rr * GRID_W, GRID_W)
        for g in range(NA_WIDTH // MXU_DIM):
            cols = slice(g * MXU_DIM, (g + 1) * MXU_DIM)
            qg = q_ref[pl.ds(qoff, GRID_W), cols]
            kg = kw_ref[pl.ds(koff, win), cols]
            vg = vw_ref[pl.ds(koff, win), cols]
            acc = jnp.zeros((GRID_W, MXU_DIM), F32)
            for hh in range(heads_per_group):
                qh = jnp.where(head_masks[hh], qg, jnp.zeros_like(qg))
                s = lax.dot_general(qh, kg, (((1,), (1,)), ((), ())),
                                    preferred_element_type=F32)
                s = s + bias_ref[delta, g * heads_per_group + hh]
                m = jnp.max(s, axis=-1, keepdims=True)
                p = jnp.exp(s - m)
                l = jnp.sum(p, axis=-1, keepdims=True)
                o = jnp.dot(p.astype(BF16), vg, preferred_element_type=F32)
                acc = jnp.where(head_masks[hh], o * (1.0 / l), acc)
            o_ref[pl.ds(qoff, GRID_W), cols] = acc.astype(BF16)
        return carry

    lax.fori_loop(0, ATTN_ROWS, row_body, 0)


def _attention(q, k, v, bias):
    b, t, w = q.shape
    rows = t // GRID_W
    assert rows % ATTN_ROWS == 0 and rows >= WIN_ROWS and ATTN_ROWS >= WIN_ROWS // 2
    nblk = rows // ATTN_ROWS
    tb = ATTN_ROWS * GRID_W
    cur = pl.BlockSpec((None, tb, w), lambda bi, i: (bi, i, 0))
    prev = pl.BlockSpec((None, tb, w), lambda bi, i: (bi, jnp.maximum(i - 1, 0), 0))
    nxt = pl.BlockSpec((None, tb, w), lambda bi, i: (bi, jnp.minimum(i + 1, nblk - 1), 0))
    return pl.pallas_call(
        functools.partial(_attn_kernel, rows=rows),
        grid=(b, nblk),
        in_specs=[cur, prev, cur, nxt, prev, cur, nxt, _resident(bias.shape)],
        out_specs=cur,
        out_shape=jax.ShapeDtypeStruct((b, t, w), BF16),
        scratch_shapes=[pltpu.VMEM((3 * tb, w), BF16), pltpu.VMEM((3 * tb, w), BF16)],
        name="nattn",
        compiler_params=pltpu.CompilerParams(
            dimension_semantics=("parallel", "parallel"),
            vmem_limit_bytes=VMEM_LIMIT_BYTES),
    )(q, k, k, k, v, v, v, bias)


def _attn_bias_table(rpb):
    c = jnp.arange(GRID_W)
    kc = jnp.arange(GRID_W)
    cs = jnp.clip(c - WIN_COLS // 2, 0, GRID_W - WIN_COLS)
    valid = (kc[None, :] >= cs[:, None]) & (kc[None, :] < cs[:, None] + WIN_COLS)
    dcol = jnp.clip(kc[None, :] - c[:, None] + (WIN_COLS - 1), 0, 2 * WIN_COLS - 2)
    drow = jnp.arange(WIN_ROWS)[None, :] - jnp.arange(WIN_ROWS)[:, None] + (WIN_ROWS - 1)
    t = rpb.astype(F32)[:, drow]
    t = t[:, :, :, dcol]
    t = jnp.where(valid[None, None, None], t, MASK_VALUE)
    t = jnp.transpose(t, (1, 0, 3, 2, 4))
    return t.reshape(WIN_ROWS, NA_HEADS, GRID_W, WIN_ROWS * GRID_W)


def _mix_kernel(x_ref, attn_ref, u_ref, vn_ref, sa_ref, sb_ref, ada_ref,
                ws_ref, bs_ref, wau_ref, wsu_ref, wo_ref, g_ref, b_ref,
                o_ref, sgu_ref, *, alpha):
    tm = x_ref.shape[0]
    first_of_pair = lax.broadcasted_iota(jnp.int32, (CHUNK, LANES), 1) < SGU_GROUP_DIM
    groups_per_vreg = LANES // SGU_GROUP_DIM
    for c in range(tm // CHUNK):
        rws = slice(c * CHUNK, (c + 1) * CHUNK)
        parts = []
        for j in range(SGU_WIDTH // LANES):
            vj = vn_ref[rws, j * LANES:(j + 1) * LANES]
            o0 = jnp.dot(ws_ref[groups_per_vreg * j], vj, preferred_element_type=F32)
            o1 = jnp.dot(ws_ref[groups_per_vreg * j + 1], vj, preferred_element_type=F32)
            parts.append(jnp.where(first_of_pair, o0, o1))
        s = jnp.concatenate(parts, axis=-1) + bs_ref[...]
        sgu_ref[rws, :] = (u_ref[rws, :].astype(F32) * s).astype(BF16)
    a = jnp.dot(attn_ref[...], wau_ref[...], preferred_element_type=F32)
    sg = jnp.dot(sgu_ref[...], wsu_ref[...], preferred_element_type=F32)
    merged = sa_ref[...].astype(F32) * a + sb_ref[...].astype(F32) * sg
    mix = jnp.dot(merged.astype(BF16), wo_ref[...], preferred_element_type=F32)
    gate = ada_ref[2:3, :]
    o_ref[...] = _layer_norm(alpha * x_ref[...] + gate * mix, g_ref[...], b_ref[...])


def _mix(x, attn, u, vn, sa, sb, ada, w_s, bs_full, w_au, w_su, w_o, ln_g, ln_b, alpha):
    b, t, d = x.shape
    tm = TOKEN_TILE
    tok = lambda w: pl.BlockSpec((None, tm, w), lambda bi, i: (bi, i, 0))
    return pl.pallas_call(
        functools.partial(_mix_kernel, alpha=alpha),
        grid=(b, t // tm),
        in_specs=[tok(d), tok(NA_WIDTH), tok(SGU_WIDTH), tok(SGU_WIDTH), tok(d), tok(d),
                  pl.BlockSpec((None, N_ADA, d), lambda bi, i: (bi, 0, 0)),
                  _resident(w_s.shape), _resident(bs_full.shape),
                  _resident(w_au.shape), _resident(w_su.shape), _resident(w_o.shape),
                  _resident((1, d)), _resident((1, d))],
        out_specs=tok(d),
        out_shape=jax.ShapeDtypeStruct((b, t, d), F32),
        scratch_shapes=[pltpu.VMEM((tm, SGU_WIDTH), BF16)],
        name="mix",
        compiler_params=pltpu.CompilerParams(
            dimension_semantics=("parallel", "parallel"),
            vmem_limit_bytes=VMEM_LIMIT_BYTES),
    )(x, attn, u, vn, sa, sb, ada, w_s, bs_full, w_au, w_su, w_o,
      ln_g.reshape(1, d), ln_b.reshape(1, d))


def _ffn_kernel(x_ref, ada_ref, w1_ref, b1_ref, w2_ref, b2_ref, g_ref, b_ref, o_ref,
                *, alpha):
    shift = ada_ref[3:4, :]
    scale = ada_ref[4:5, :]
    gate = ada_ref[5:6, :]
    x = x_ref[...]
    h = (x * (1.0 + scale) + shift).astype(BF16)
    f = jnp.zeros(x.shape, F32)
    for c in range(w1_ref.shape[1] // FF_CHUNK):
        cols = slice(c * FF_CHUNK, (c + 1) * FF_CHUNK)
        t = jnp.dot(h, w1_ref[:, cols], preferred_element_type=F32) + b1_ref[:, cols]
        t = jnp.maximum(t, 0.0)
        f = f + jnp.dot((t * t).astype(BF16), w2_ref[cols, :], preferred_element_type=F32)
    f = f + b2_ref[...]
    o_ref[...] = _layer_norm(alpha * x + gate * f, g_ref[...], b_ref[...])


def _ffn(x, ada, w1, b1, w2, b2, ln_g, ln_b, alpha):
    b, t, d = x.shape
    dff = w1.shape[1]
    tm = TOKEN_TILE
    tok = pl.BlockSpec((None, tm, d), lambda bi, i: (bi, i, 0))
    return pl.pallas_call(
        functools.partial(_ffn_kernel, alpha=alpha),
        grid=(b, t // tm),
        in_specs=[tok,
                  pl.BlockSpec((None, N_ADA, d), lambda bi, i: (bi, 0, 0)),
                  _resident(w1.shape), _resident((1, dff)),
                  _resident(w2.shape), _resident((1, d)),
                  _resident((1, d)), _resident((1, d))],
        out_specs=tok,
        out_shape=jax.ShapeDtypeStruct((b, t, d), F32),
        name="ffn",
        compiler_params=pltpu.CompilerParams(
            dimension_semantics=("parallel", "parallel"),
            vmem_limit_bytes=VMEM_LIMIT_BYTES),
    )(x, ada, w1, b1.reshape(1, dff), w2, b2.reshape(1, d),
      ln_g.reshape(1, d), ln_b.reshape(1, d))


def _encoder_layer(x, ada, p, alpha):
    q, k, v, u, vn, sa, sb = _inproj(x, ada, p["w_in"], p["sgu_ln_g"], p["sgu_ln_b"])
    attn = _attention(q, k, v, p["bias"])
    x = _mix(x, attn, u, vn, sa, sb, ada, p["w_s"], p["bs_full"], p["w_attn_up"],
             p["w_sgu_up"], p["w_o"], p["ln1_g"], p["ln1_b"], alpha)
    return _ffn(x, ada, p["w_ff1"], p["b_ff1"], p["w_ff2"], p["b_ff2"],
                p["ln2_g"], p["ln2_b"], alpha)


def kernel(x_prompt, x_sample, c_prompt, c_sample, w_ada, b_ada, w_in, rpb, sgu_ln_g, sgu_ln_b, w_s, b_s, w_attn_up, w_sgu_up, w_o, ln1_g, ln1_b, w_ff1, b_ff1, w_ff2, b_ff2, ln2_g, ln2_b):
    depth, d = w_ada.shape[0], w_ada.shape[1]
    alpha = float((2.0 * depth) ** 0.25)
    nbp = c_prompt.shape[0]
    c_all = jnp.concatenate([c_prompt, c_sample], axis=0)
    xs = [x_prompt, x_sample]
    for l in range(depth):
        p = {
            "w_in": w_in[l].astype(BF16),
            "sgu_ln_g": sgu_ln_g[l], "sgu_ln_b": sgu_ln_b[l],
            "bias": _attn_bias_table(rpb[l]),
            "w_s": w_s[l].astype(BF16),
            "bs_full": jnp.repeat(jnp.transpose(b_s[l]), SGU_GROUP_DIM, axis=1),
            "w_attn_up": w_attn_up[l].astype(BF16),
            "w_sgu_up": w_sgu_up[l].astype(BF16),
            "w_o": w_o[l].astype(BF16),
            "ln1_g": ln1_g[l], "ln1_b": ln1_b[l],
            "w_ff1": w_ff1[l].astype(BF16), "b_ff1": b_ff1[l],
            "w_ff2": w_ff2[l].astype(BF16), "b_ff2": b_ff2[l],
            "ln2_g": ln2_g[l], "ln2_b": ln2_b[l],
        }
        ada = _ada(c_all, w_ada[l], b_ada[l]).reshape(-1, N_ADA, d)
        xs = [_encoder_layer(xs[0], ada[:nbp], p, alpha),
              _encoder_layer(xs[1], ada[nbp:], p, alpha)]
    return (xs[0], xs[1])
```

```python
import functools

import numpy as np
import jax
import jax.numpy as jnp
from jax import lax
from jax.experimental import pallas as pl
from jax.experimental.pallas import tpu as pltpu

GRID_W = 64
NA_HEADS = 8
NA_HEAD_DIM = 64
NA_WIDTH = NA_HEADS * NA_HEAD_DIM
WIN_ROWS = 8
WIN_COLS = 16
SGU_GROUPS = 8
SGU_GROUP_DIM = 64
SGU_WIDTH = SGU_GROUPS * SGU_GROUP_DIM
CHUNK = 128
N_ADA = 6
LN_EPS = 1e-5
MASK_VALUE = -1e30

MXU_DIM = 256
LANES = 128
VMEM_LIMIT_BYTES = 48 * 1024 * 1024

TOKEN_TILE = 512
ATTN_ROWS = 8
ATTN_UNROLL = 8
FF_CHUNK = 1024

BF16 = jnp.bfloat16
F32 = jnp.float32
_SQRT_2_OVER_PI = np.float32(np.sqrt(2.0 / np.pi))


def _gelu_tanh(x):
    return x * (0.5 * (1.0 + jnp.tanh(_SQRT_2_OVER_PI * (x + 0.044715 * (x * x * x)))))


def _layer_norm(x, g, b):
    mu = jnp.mean(x, axis=-1, keepdims=True)
    xc = x - mu
    var = jnp.mean(xc * xc, axis=-1, keepdims=True)
    return xc * lax.rsqrt(var + LN_EPS) * g + b


def _resident(shape):
    zeros = (0,) * len(shape)
    return pl.BlockSpec(shape, lambda *_: zeros, pipeline_mode=pl.Buffered(1))


def _ada_kernel(c_ref, w_ref, b_ref, o_ref):
    c = c_ref[...]
    s = c * jax.nn.sigmoid(c)
    o_ref[...] = jnp.dot(s, w_ref[...], preferred_element_type=F32,
                         precision=lax.Precision.HIGHEST) + b_ref[...]


def _ada(c, w_ada, b_ada):
    nb, d = c.shape
    n = w_ada.shape[1]
    return pl.pallas_call(
        _ada_kernel,
        grid=(n // d,),
        in_specs=[pl.BlockSpec((nb, d), lambda j: (0, 0)),
                  pl.BlockSpec((d, d), lambda j: (0, j)),
                  pl.BlockSpec((1, d), lambda j: (0, j))],
        out_specs=pl.BlockSpec((nb, d), lambda j: (0, j)),
        out_shape=jax.ShapeDtypeStruct((nb, n), F32),
        name="ada",
        compiler_params=pltpu.CompilerParams(vmem_limit_bytes=VMEM_LIMIT_BYTES),
    )(c, w_ada, b_ada.reshape(1, n))


def _inproj_kernel(x_ref, ada_ref, w_ref, lng_ref, lnb_ref,
                   q_ref, k_ref, v_ref, u_ref, vn_ref, sa_ref, sb_ref):
    shift = ada_ref[0:1, :]
    scale = ada_ref[1:2, :]
    h = (x_ref[...] * (1.0 + scale) + shift).astype(BF16)

    def seg(lo, width):
        return jnp.dot(h, w_ref[:, lo:lo + width], preferred_element_type=F32)

    q_ref[...] = (seg(0, NA_WIDTH) * (NA_HEAD_DIM ** -0.5)).astype(BF16)
    k_ref[...] = seg(NA_WIDTH, NA_WIDTH).astype(BF16)
    v_ref[...] = seg(2 * NA_WIDTH, NA_WIDTH).astype(BF16)
    lo = 3 * NA_WIDTH
    u_ref[...] = _gelu_tanh(seg(lo, SGU_WIDTH)).astype(BF16)
    lo += SGU_WIDTH
    vs = _gelu_tanh(seg(lo, SGU_WIDTH))
    vn_ref[...] = _layer_norm(vs, lng_ref[...], lnb_ref[...]).astype(BF16)
    lo += SGU_WIDTH
    d = sa_ref.shape[-1]
    sa_ref[...] = jax.nn.sigmoid(seg(lo, d)).astype(BF16)
    sb_ref[...] = jax.nn.sigmoid(seg(lo + d, d)).astype(BF16)


def _inproj(x, ada, w_in, ln_g, ln_b):
    b, t, d = x.shape
    tm = TOKEN_TILE
    tok = lambda w: pl.BlockSpec((None, tm, w), lambda bi, i: (bi, i, 0))
    widths = (NA_WIDTH, NA_WIDTH, NA_WIDTH, SGU_WIDTH, SGU_WIDTH, d, d)
    return pl.pallas_call(
        _inproj_kernel,
        grid=(b, t // tm),
        in_specs=[tok(d),
                  pl.BlockSpec((None, N_ADA, d), lambda bi, i: (bi, 0, 0)),
                  _resident(w_in.shape),
                  _resident((1, SGU_WIDTH)),
                  _resident((1, SGU_WIDTH))],
        out_specs=[tok(w) for w in widths],
        out_shape=[jax.ShapeDtypeStruct((b, t, w), BF16) for w in widths],
        name="inproj",
        compiler_params=pltpu.CompilerParams(
            dimension_semantics=("parallel", "parallel"),
            vmem_limit_bytes=VMEM_LIMIT_BYTES),
    )(x, ada, w_in, ln_g.reshape(1, -1), ln_b.reshape(1, -1))


def _attn_kernel(q_ref, kp_ref, kc_ref, kn_ref, vp_ref, vc_ref, vn_ref, bias_ref, hmask_ref,
                 o_ref, kw_ref, vw_ref, *, rows):
    i = pl.program_id(1)
    tb = ATTN_ROWS * GRID_W
    win = WIN_ROWS * GRID_W
    kw_ref[0:tb, :] = kp_ref[...]
    kw_ref[tb:2 * tb, :] = kc_ref[...]
    kw_ref[2 * tb:3 * tb, :] = kn_ref[...]
    vw_ref[0:tb, :] = vp_ref[...]
    vw_ref[tb:2 * tb, :] = vc_ref[...]
    vw_ref[2 * tb:3 * tb, :] = vn_ref[...]

    heads_per_group = MXU_DIM // NA_HEAD_DIM
    heads_per_vreg = LANES // NA_HEAD_DIM
    first_head_lanes = lax.broadcasted_iota(jnp.int32, (GRID_W, LANES), 1) < NA_HEAD_DIM

    def row_body(rr, carry):
        r = i * ATTN_ROWS + rr
        rs = jnp.clip(r - WIN_ROWS // 2, 0, rows - WIN_ROWS)
        delta = r - rs
        koff = pl.multiple_of((rs - (i - 1) * ATTN_ROWS) * GRID_W, GRID_W)
        qoff = pl.multiple_of(rr * GRID_W, GRID_W)
        for g in range(NA_WIDTH // MXU_DIM):
            cols = slice(g * MXU_DIM, (g + 1) * MXU_DIM)
            qg = q_ref[pl.ds(qoff, GRID_W), cols]
            kg = kw_ref[pl.ds(koff, win), cols]
            vg = vw_ref[pl.ds(koff, win), cols]
            qs = jnp.concatenate([qg] * heads_per_group, axis=0) * hmask_ref[...]
            s = lax.dot_general(qs, kg, (((1,), (1,)), ((), ())),
                                preferred_element_type=F32)
            s = s + bias_ref[delta, g]
            m = jnp.max(s, axis=-1, keepdims=True)
            p = jnp.exp(s - m)
            l = jnp.sum(p, axis=-1, keepdims=True)
            o = jnp.dot(p.astype(BF16), vg, preferred_element_type=F32) * (1.0 / l)
            for j in range(MXU_DIM // LANES):
                blk = [o[(heads_per_vreg * j + e) * GRID_W:(heads_per_vreg * j + e + 1) * GRID_W,
                         j * LANES:(j + 1) * LANES] for e in range(heads_per_vreg)]
                o_ref[pl.ds(qoff, GRID_W), pl.ds(g * MXU_DIM + j * LANES, LANES)] = (
                    jnp.where(first_head_lanes, blk[0], blk[1]).astype(BF16))
        return carry

    lax.fori_loop(0, ATTN_ROWS, row_body, 0, unroll=ATTN_UNROLL)


def _attention(q, k, v, bias):
    b, t, w = q.shape
    rows = t // GRID_W
    assert rows % ATTN_ROWS == 0 and rows >= WIN_ROWS and ATTN_ROWS >= WIN_ROWS // 2
    nblk = rows // ATTN_ROWS
    tb = ATTN_ROWS * GRID_W
    cur = pl.BlockSpec((None, tb, w), lambda bi, i: (bi, i, 0))
    prev = pl.BlockSpec((None, tb, w), lambda bi, i: (bi, jnp.maximum(i - 1, 0), 0))
    nxt = pl.BlockSpec((None, tb, w), lambda bi, i: (bi, jnp.minimum(i + 1, nblk - 1), 0))
    head_of = np.arange(MXU_DIM) // NA_HEAD_DIM
    hmask = jnp.asarray(head_of[:, None] == head_of[None, :], dtype=BF16)
    return pl.pallas_call(
        functools.partial(_attn_kernel, rows=rows),
        grid=(b, nblk),
        in_specs=[cur, prev, cur, nxt, prev, cur, nxt, _resident(bias.shape),
                  _resident(hmask.shape)],
        out_specs=cur,
        out_shape=jax.ShapeDtypeStruct((b, t, w), BF16),
        scratch_shapes=[pltpu.VMEM((3 * tb, w), BF16), pltpu.VMEM((3 * tb, w), BF16)],
        name="nattn",
        compiler_params=pltpu.CompilerParams(
            dimension_semantics=("parallel", "parallel"),
            vmem_limit_bytes=VMEM_LIMIT_BYTES),
    )(q, k, k, k, v, v, v, bias, hmask)


def _attn_bias_table(rpb):
    c = np.arange(GRID_W)
    kc = np.arange(GRID_W)
    cs = np.clip(c - WIN_COLS // 2, 0, GRID_W - WIN_COLS)
    valid = (kc[None, :] >= cs[:, None]) & (kc[None, :] < cs[:, None] + WIN_COLS)
    dcol = np.clip(kc[None, :] - c[:, None] + (WIN_COLS - 1), 0, 2 * WIN_COLS - 2)
    planes = jnp.where(valid, rpb.astype(F32)[:, :, dcol], MASK_VALUE)
    planes = jnp.transpose(planes, (0, 2, 1, 3))
    t = jnp.stack([planes[:, :, WIN_ROWS - 1 - dl:2 * WIN_ROWS - 1 - dl]
                   for dl in range(WIN_ROWS)])
    return t.reshape(WIN_ROWS, NA_WIDTH // MXU_DIM, MXU_DIM, WIN_ROWS * GRID_W)


def _mix_kernel(x_ref, attn_ref, u_ref, vn_ref, sa_ref, sb_ref, ada_ref,
                ws_ref, bs_ref, wau_ref, wsu_ref, wo_ref, g_ref, b_ref,
                o_ref, sgu_ref, *, alpha):
    tm = x_ref.shape[0]
    first_of_pair = lax.broadcasted_iota(jnp.int32, (CHUNK, LANES), 1) < SGU_GROUP_DIM
    groups_per_vreg = LANES // SGU_GROUP_DIM
    for c in range(tm // CHUNK):
        rws = slice(c * CHUNK, (c + 1) * CHUNK)
        parts = []
        for j in range(SGU_WIDTH // LANES):
            vj = vn_ref[rws, j * LANES:(j + 1) * LANES]
            o0 = jnp.dot(ws_ref[groups_per_vreg * j], vj, preferred_element_type=F32)
            o1 = jnp.dot(ws_ref[groups_per_vreg * j + 1], vj, preferred_element_type=F32)
            parts.append(jnp.where(first_of_pair, o0, o1))
        s = jnp.concatenate(parts, axis=-1) + bs_ref[...]
        sgu_ref[rws, :] = (u_ref[rws, :].astype(F32) * s).astype(BF16)
    a = jnp.dot(attn_ref[...], wau_ref[...], preferred_element_type=F32)
    sg = jnp.dot(sgu_ref[...], wsu_ref[...], preferred_element_type=F32)
    merged = sa_ref[...].astype(F32) * a + sb_ref[...].astype(F32) * sg
    mix = jnp.dot(merged.astype(BF16), wo_ref[...], preferred_element_type=F32)
    gate = ada_ref[2:3, :]
    o_ref[...] = _layer_norm(alpha * x_ref[...] + gate * mix, g_ref[...], b_ref[...])


def _mix(x, attn, u, vn, sa, sb, ada, w_s, bs_full, w_au, w_su, w_o, ln_g, ln_b, alpha):
    b, t, d = x.shape
    tm = TOKEN_TILE
    tok = lambda w: pl.BlockSpec((None, tm, w), lambda bi, i: (bi, i, 0))
    return pl.pallas_call(
        functools.partial(_mix_kernel, alpha=alpha),
        grid=(b, t // tm),
        in_specs=[tok(d), tok(NA_WIDTH), tok(SGU_WIDTH), tok(SGU_WIDTH), tok(d), tok(d),
                  pl.BlockSpec((None, N_ADA, d), lambda bi, i: (bi, 0, 0)),
                  _resident(w_s.shape), _resident(bs_full.shape),
                  _resident(w_au.shape), _resident(w_su.shape), _resident(w_o.shape),
                  _resident((1, d)), _resident((1, d))],
        out_specs=tok(d),
        out_shape=jax.ShapeDtypeStruct((b, t, d), F32),
        scratch_shapes=[pltpu.VMEM((tm, SGU_WIDTH), BF16)],
        name="mix",
        compiler_params=pltpu.CompilerParams(
            dimension_semantics=("parallel", "parallel"),
            vmem_limit_bytes=VMEM_LIMIT_BYTES),
    )(x, attn, u, vn, sa, sb, ada, w_s, bs_full, w_au, w_su, w_o,
      ln_g.reshape(1, d), ln_b.reshape(1, d))


def _ffn_kernel(x_ref, ada_ref, w1_ref, b1_ref, w2_ref, b2_ref, g_ref, b_ref, o_ref,
                *, alpha):
    shift = ada_ref[3:4, :]
    scale = ada_ref[4:5, :]
    gate = ada_ref[5:6, :]
    x = x_ref[...]
    h = (x * (1.0 + scale) + shift).astype(BF16)
    f = jnp.zeros(x.shape, F32)
    for c in range(w1_ref.shape[1] // FF_CHUNK):
        cols = slice(c * FF_CHUNK, (c + 1) * FF_CHUNK)
        t = jnp.dot(h, w1_ref[:, cols], preferred_element_type=F32) + b1_ref[:, cols]
        t = jnp.maximum(t, 0.0)
        f = f + jnp.dot((t * t).astype(BF16), w2_ref[cols, :], preferred_element_type=F32)
    f = f + b2_ref[...]
    o_ref[...] = _layer_norm(alpha * x + gate * f, g_ref[...], b_ref[...])


def _ffn(x, ada, w1, b1, w2, b2, ln_g, ln_b, alpha):
    b, t, d = x.shape
    dff = w1.shape[1]
    tm = TOKEN_TILE
    tok = pl.BlockSpec((None, tm, d), lambda bi, i: (bi, i, 0))
    return pl.pallas_call(
        functools.partial(_ffn_kernel, alpha=alpha),
        grid=(b, t // tm),
        in_specs=[tok,
                  pl.BlockSpec((None, N_ADA, d), lambda bi, i: (bi, 0, 0)),
                  _resident(w1.shape), _resident((1, dff)),
                  _resident(w2.shape), _resident((1, d)),
                  _resident((1, d)), _resident((1, d))],
        out_specs=tok,
        out_shape=jax.ShapeDtypeStruct((b, t, d), F32),
        name="ffn",
        compiler_params=pltpu.CompilerParams(
            dimension_semantics=("parallel", "parallel"),
            vmem_limit_bytes=VMEM_LIMIT_BYTES),
    )(x, ada, w1, b1.reshape(1, dff), w2, b2.reshape(1, d),
      ln_g.reshape(1, d), ln_b.reshape(1, d))


def _encoder_layer(x, ada, p, alpha):
    q, k, v, u, vn, sa, sb = _inproj(x, ada, p["w_in"], p["sgu_ln_g"], p["sgu_ln_b"])
    attn = _attention(q, k, v, p["bias"])
    x = _mix(x, attn, u, vn, sa, sb, ada, p["w_s"], p["bs_full"], p["w_attn_up"],
             p["w_sgu_up"], p["w_o"], p["ln1_g"], p["ln1_b"], alpha)
    return _ffn(x, ada, p["w_ff1"], p["b_ff1"], p["w_ff2"], p["b_ff2"],
                p["ln2_g"], p["ln2_b"], alpha)


def kernel(x_prompt, x_sample, c_prompt, c_sample, w_ada, b_ada, w_in, rpb, sgu_ln_g, sgu_ln_b, w_s, b_s, w_attn_up, w_sgu_up, w_o, ln1_g, ln1_b, w_ff1, b_ff1, w_ff2, b_ff2, ln2_g, ln2_b):
    depth, d = w_ada.shape[0], w_ada.shape[1]
    alpha = float((2.0 * depth) ** 0.25)
    nbp = c_prompt.shape[0]
    c_all = jnp.concatenate([c_prompt, c_sample], axis=0)
    xs = [x_prompt, x_sample]
    for l in range(depth):
        p = {
            "w_in": w_in[l].astype(BF16),
            "sgu_ln_g": sgu_ln_g[l], "sgu_ln_b": sgu_ln_b[l],
            "bias": _attn_bias_table(rpb[l]),
            "w_s": w_s[l].astype(BF16),
            "bs_full": jnp.repeat(jnp.transpose(b_s[l]), SGU_GROUP_DIM, axis=1),
            "w_attn_up": w_attn_up[l].astype(BF16),
            "w_sgu_up": w_sgu_up[l].astype(BF16),
            "w_o": w_o[l].astype(BF16),
            "ln1_g": ln1_g[l], "ln1_b": ln1_b[l],
            "w_ff1": w_ff1[l].astype(BF16), "b_ff1": b_ff1[l],
            "w_ff2": w_ff2[l].astype(BF16), "b_ff2": b_ff2[l],
            "ln2_g": ln2_g[l], "ln2_b": ln2_b[l],
        }
        ada = _ada(c_all, w_ada[l], b_ada[l]).reshape(-1, N_ADA, d)
        xs = [_encoder_layer(xs[0], ada[:nbp], p, alpha),
              _encoder_layer(xs[1], ada[nbp:], p, alpha)]
    return (xs[0], xs[1])
```

```python
import functools

import numpy as np
import jax
import jax.numpy as jnp
from jax import lax
from jax.experimental import pallas as pl
from jax.experimental.pallas import tpu as pltpu

GRID_W = 64
NA_HEADS = 8
NA_HEAD_DIM = 64
NA_WIDTH = NA_HEADS * NA_HEAD_DIM
WIN_ROWS = 8
WIN_COLS = 16
SGU_GROUPS = 8
SGU_GROUP_DIM = 64
SGU_WIDTH = SGU_GROUPS * SGU_GROUP_DIM
CHUNK = 128
N_ADA = 6
LN_EPS = 1e-5
MASK_VALUE = -1e30

MXU_DIM = 256
LANES = 128
VMEM_LIMIT_BYTES = 48 * 1024 * 1024

TOKEN_TILE = 1024
SUB_TILE = 2 * CHUNK
ATTN_ROWS = 8
ATTN_UNROLL = 8
FF_CHUNK = 1024

BF16 = jnp.bfloat16
F32 = jnp.float32
_SQRT_2_OVER_PI = np.float32(np.sqrt(2.0 / np.pi))


def _gelu_tanh(x):
    return x * (0.5 * (1.0 + jnp.tanh(_SQRT_2_OVER_PI * (x + 0.044715 * (x * x * x)))))


def _layer_norm(x, g, b):
    mu = jnp.mean(x, axis=-1, keepdims=True)
    xc = x - mu
    var = jnp.mean(xc * xc, axis=-1, keepdims=True)
    return xc * lax.rsqrt(var + LN_EPS) * g + b


def _resident(shape):
    zeros = (0,) * len(shape)
    return pl.BlockSpec(shape, lambda *_: zeros, pipeline_mode=pl.Buffered(1))


def _ada_kernel(c_ref, w_ref, b_ref, o_ref):
    c = c_ref[...]
    s = c * jax.nn.sigmoid(c)
    o_ref[...] = jnp.dot(s, w_ref[...], preferred_element_type=F32,
                         precision=lax.Precision.HIGHEST) + b_ref[...]


def _ada(c, w_ada, b_ada):
    nb, d = c.shape
    n = w_ada.shape[1]
    return pl.pallas_call(
        _ada_kernel,
        grid=(n // d,),
        in_specs=[pl.BlockSpec((nb, d), lambda j: (0, 0)),
                  pl.BlockSpec((d, d), lambda j: (0, j)),
                  pl.BlockSpec((1, d), lambda j: (0, j))],
        out_specs=pl.BlockSpec((nb, d), lambda j: (0, j)),
        out_shape=jax.ShapeDtypeStruct((nb, n), F32),
        name="ada",
        compiler_params=pltpu.CompilerParams(vmem_limit_bytes=VMEM_LIMIT_BYTES),
    )(c, w_ada, b_ada.reshape(1, n))


def _inproj_kernel(x_ref, ada_ref, w_ref, lng_ref, lnb_ref,
                   q_ref, k_ref, v_ref, u_ref, vn_ref, sa_ref, sb_ref):
    shift = ada_ref[0:1, :]
    scale = ada_ref[1:2, :]
    d = sa_ref.shape[-1]
    u_lo = 3 * NA_WIDTH
    gate_lo = u_lo + 2 * SGU_WIDTH
    for t in range(x_ref.shape[0] // SUB_TILE):
        rws = slice(t * SUB_TILE, (t + 1) * SUB_TILE)
        h = (x_ref[rws, :] * (1.0 + scale) + shift).astype(BF16)

        def seg(lo, width):
            return jnp.dot(h, w_ref[:, lo:lo + width], preferred_element_type=F32)

        sa_ref[rws, :] = jax.nn.sigmoid(seg(gate_lo, d)).astype(BF16)
        sb_ref[rws, :] = jax.nn.sigmoid(seg(gate_lo + d, d)).astype(BF16)
        vs = _gelu_tanh(seg(u_lo + SGU_WIDTH, SGU_WIDTH))
        vn_ref[rws, :] = _layer_norm(vs, lng_ref[...], lnb_ref[...]).astype(BF16)
        u_ref[rws, :] = _gelu_tanh(seg(u_lo, SGU_WIDTH)).astype(BF16)
        q_ref[rws, :] = (seg(0, NA_WIDTH) * (NA_HEAD_DIM ** -0.5)).astype(BF16)
        k_ref[rws, :] = seg(NA_WIDTH, NA_WIDTH).astype(BF16)
        v_ref[rws, :] = seg(2 * NA_WIDTH, NA_WIDTH).astype(BF16)


def _inproj(x, ada, w_in, ln_g, ln_b):
    b, t, d = x.shape
    tm = TOKEN_TILE
    tok = lambda w: pl.BlockSpec((None, tm, w), lambda bi, i: (bi, i, 0))
    widths = (NA_WIDTH, NA_WIDTH, NA_WIDTH, SGU_WIDTH, SGU_WIDTH, d, d)
    return pl.pallas_call(
        _inproj_kernel,
        grid=(b, t // tm),
        in_specs=[tok(d),
                  pl.BlockSpec((None, N_ADA, d), lambda bi, i: (bi, 0, 0)),
                  _resident(w_in.shape),
                  _resident((1, SGU_WIDTH)),
                  _resident((1, SGU_WIDTH))],
        out_specs=[tok(w) for w in widths],
        out_shape=[jax.ShapeDtypeStruct((b, t, w), BF16) for w in widths],
        name="inproj",
        compiler_params=pltpu.CompilerParams(
            dimension_semantics=("parallel", "parallel"),
            vmem_limit_bytes=VMEM_LIMIT_BYTES),
    )(x, ada, w_in, ln_g.reshape(1, -1), ln_b.reshape(1, -1))


def _attn_kernel(q_ref, kp_ref, kc_ref, kn_ref, vp_ref, vc_ref, vn_ref, bias_ref, hmask_ref,
                 o_ref, kw_ref, vw_ref, *, rows):
    i = pl.program_id(1)
    tb = ATTN_ROWS * GRID_W
    win = WIN_ROWS * GRID_W
    kw_ref[0:tb, :] = kp_ref[...]
    kw_ref[tb:2 * tb, :] = kc_ref[...]
    kw_ref[2 * tb:3 * tb, :] = kn_ref[...]
    vw_ref[0:tb, :] = vp_ref[...]
    vw_ref[tb:2 * tb, :] = vc_ref[...]
    vw_ref[2 * tb:3 * tb, :] = vn_ref[...]

    heads_per_group = MXU_DIM // NA_HEAD_DIM
    heads_per_vreg = LANES // NA_HEAD_DIM
    first_head_lanes = lax.broadcasted_iota(jnp.int32, (GRID_W, LANES), 1) < NA_HEAD_DIM

    def row_body(rr, carry):
        r = i * ATTN_ROWS + rr
        rs = jnp.clip(r - WIN_ROWS // 2, 0, rows - WIN_ROWS)
        delta = r - rs
        koff = pl.multiple_of((rs - (i - 1) * ATTN_ROWS) * GRID_W, GRID_W)
        qoff = pl.multiple_of(rr * GRID_W, GRID_W)
        for g in range(NA_WIDTH // MXU_DIM):
            cols = slice(g * MXU_DIM, (g + 1) * MXU_DIM)
            qg = q_ref[pl.ds(qoff, GRID_W), cols]
            kg = kw_ref[pl.ds(koff, win), cols]
            vg = vw_ref[pl.ds(koff, win), cols]
            qs = jnp.concatenate([qg] * heads_per_group, axis=0) * hmask_ref[...]
            s = lax.dot_general(qs, kg, (((1,), (1,)), ((), ())),
                                preferred_element_type=F32)
            s = s + bias_ref[delta, g]
            m = jnp.max(s, axis=-1, keepdims=True)
            p = jnp.exp(s - m)
            l = jnp.sum(p, axis=-1, keepdims=True)
            o = jnp.dot(p.astype(BF16), vg, preferred_element_type=F32) * (1.0 / l)
            for j in range(MXU_DIM // LANES):
                blk = [o[(heads_per_vreg * j + e) * GRID_W:(heads_per_vreg * j + e + 1) * GRID_W,
                         j * LANES:(j + 1) * LANES] for e in range(heads_per_vreg)]
                o_ref[pl.ds(qoff, GRID_W), pl.ds(g * MXU_DIM + j * LANES, LANES)] = (
                    jnp.where(first_head_lanes, blk[0], blk[1]).astype(BF16))
        return carry

    lax.fori_loop(0, ATTN_ROWS, row_body, 0, unroll=ATTN_UNROLL)


def _attention(q, k, v, bias):
    b, t, w = q.shape
    rows = t // GRID_W
    assert rows % ATTN_ROWS == 0 and rows >= WIN_ROWS and ATTN_ROWS >= WIN_ROWS // 2
    nblk = rows // ATTN_ROWS
    tb = ATTN_ROWS * GRID_W
    cur = pl.BlockSpec((None, tb, w), lambda bi, i: (bi, i, 0))
    prev = pl.BlockSpec((None, tb, w), lambda bi, i: (bi, jnp.maximum(i - 1, 0), 0))
    nxt = pl.BlockSpec((None, tb, w), lambda bi, i: (bi, jnp.minimum(i + 1, nblk - 1), 0))
    head_of = np.arange(MXU_DIM) // NA_HEAD_DIM
    hmask = jnp.asarray(head_of[:, None] == head_of[None, :], dtype=BF16)
    return pl.pallas_call(
        functools.partial(_attn_kernel, rows=rows),
        grid=(b, nblk),
        in_specs=[cur, prev, cur, nxt, prev, cur, nxt, _resident(bias.shape),
                  _resident(hmask.shape)],
        out_specs=cur,
        out_shape=jax.ShapeDtypeStruct((b, t, w), BF16),
        scratch_shapes=[pltpu.VMEM((3 * tb, w), BF16), pltpu.VMEM((3 * tb, w), BF16)],
        name="nattn",
        compiler_params=pltpu.CompilerParams(
            dimension_semantics=("parallel", "parallel"),
            vmem_limit_bytes=VMEM_LIMIT_BYTES),
    )(q, k, k, k, v, v, v, bias, hmask)


def _attn_bias_table(rpb):
    c = np.arange(GRID_W)
    kc = np.arange(GRID_W)
    cs = np.clip(c - WIN_COLS // 2, 0, GRID_W - WIN_COLS)
    valid = (kc[None, :] >= cs[:, None]) & (kc[None, :] < cs[:, None] + WIN_COLS)
    dcol = np.clip(kc[None, :] - c[:, None] + (WIN_COLS - 1), 0, 2 * WIN_COLS - 2)
    planes = jnp.where(valid, rpb.astype(F32)[:, :, dcol], MASK_VALUE)
    planes = jnp.transpose(planes, (0, 2, 1, 3))
    t = jnp.stack([planes[:, :, WIN_ROWS - 1 - dl:2 * WIN_ROWS - 1 - dl]
                   for dl in range(WIN_ROWS)])
    return t.reshape(WIN_ROWS, NA_WIDTH // MXU_DIM, MXU_DIM, WIN_ROWS * GRID_W)


def _mix_kernel(x_ref, attn_ref, u_ref, vn_ref, sa_ref, sb_ref, ada_ref,
                ws_ref, bs_ref, wau_ref, wsu_ref, wo_ref, g_ref, b_ref,
                o_ref, *, alpha):
    gate = ada_ref[2:3, :]
    lane = lax.broadcasted_iota(jnp.int32, (CHUNK, LANES), 1)
    lo_lanes = jnp.where(lane < SGU_GROUP_DIM, 1.0, 0.0).astype(BF16)
    hi_lanes = jnp.where(lane < SGU_GROUP_DIM, 0.0, 1.0).astype(BF16)
    for t in range(x_ref.shape[0] // SUB_TILE):
        ra = slice(t * SUB_TILE, t * SUB_TILE + CHUNK)
        rb = slice(t * SUB_TILE + CHUNK, (t + 1) * SUB_TILE)
        rws = slice(t * SUB_TILE, (t + 1) * SUB_TILE)
        s_chunk_a, s_chunk_b = [], []
        for j in range(SGU_WIDTH // LANES):
            cols = slice(j * LANES, (j + 1) * LANES)
            va = vn_ref[ra, cols]
            vb = vn_ref[rb, cols]
            rhs = jnp.concatenate(
                [jnp.concatenate([va * lo_lanes, vb * lo_lanes], axis=1),
                 jnp.concatenate([va * hi_lanes, vb * hi_lanes], axis=1)], axis=0)
            sj = jnp.dot(ws_ref[j], rhs, preferred_element_type=F32)
            s_chunk_a.append(sj[:, :LANES])
            s_chunk_b.append(sj[:, LANES:])
        bs = bs_ref[...]
        s = jnp.concatenate([jnp.concatenate(s_chunk_a, axis=1) + bs,
                             jnp.concatenate(s_chunk_b, axis=1) + bs], axis=0)
        sgu = (u_ref[rws, :].astype(F32) * s).astype(BF16)
        a = jnp.dot(attn_ref[rws, :], wau_ref[...], preferred_element_type=F32)
        sg = jnp.dot(sgu, wsu_ref[...], preferred_element_type=F32)
        merged = sa_ref[rws, :].astype(F32) * a + sb_ref[rws, :].astype(F32) * sg
        mix = jnp.dot(merged.astype(BF16), wo_ref[...], preferred_element_type=F32)
        o_ref[rws, :] = _layer_norm(alpha * x_ref[rws, :] + gate * mix,
                                    g_ref[...], b_ref[...])


def _mix(x, attn, u, vn, sa, sb, ada, w_s, bs_full, w_au, w_su, w_o, ln_g, ln_b, alpha):
    b, t, d = x.shape
    tm = TOKEN_TILE
    tok = lambda w: pl.BlockSpec((None, tm, w), lambda bi, i: (bi, i, 0))
    return pl.pallas_call(
        functools.partial(_mix_kernel, alpha=alpha),
        grid=(b, t // tm),
        in_specs=[tok(d), tok(NA_WIDTH), tok(SGU_WIDTH), tok(SGU_WIDTH), tok(d), tok(d),
                  pl.BlockSpec((None, N_ADA, d), lambda bi, i: (bi, 0, 0)),
                  _resident(w_s.shape), _resident(bs_full.shape),
                  _resident(w_au.shape), _resident(w_su.shape), _resident(w_o.shape),
                  _resident((1, d)), _resident((1, d))],
        out_specs=tok(d),
        out_shape=jax.ShapeDtypeStruct((b, t, d), F32),
        name="mix",
        compiler_params=pltpu.CompilerParams(
            dimension_semantics=("parallel", "parallel"),
            vmem_limit_bytes=VMEM_LIMIT_BYTES),
    )(x, attn, u, vn, sa, sb, ada, w_s, bs_full, w_au, w_su, w_o,
      ln_g.reshape(1, d), ln_b.reshape(1, d))


def _ffn_kernel(x_ref, ada_ref, w1_ref, b1_ref, w2_ref, b2_ref, g_ref, b_ref, o_ref,
                *, alpha):
    shift = ada_ref[3:4, :]
    scale = ada_ref[4:5, :]
    gate = ada_ref[5:6, :]
    for s in range(x_ref.shape[0] // SUB_TILE):
        rws = slice(s * SUB_TILE, (s + 1) * SUB_TILE)
        x = x_ref[rws, :]
        h = (x * (1.0 + scale) + shift).astype(BF16)
        f = jnp.zeros(x.shape, F32)
        for c in range(w1_ref.shape[1] // FF_CHUNK):
            cols = slice(c * FF_CHUNK, (c + 1) * FF_CHUNK)
            t = jnp.dot(h, w1_ref[:, cols], preferred_element_type=F32) + b1_ref[:, cols]
            t = jnp.maximum(t, 0.0)
            f = f + jnp.dot((t * t).astype(BF16), w2_ref[cols, :],
                            preferred_element_type=F32)
        f = f + b2_ref[...]
        o_ref[rws, :] = _layer_norm(alpha * x + gate * f, g_ref[...], b_ref[...])


def _ffn(x, ada, w1, b1, w2, b2, ln_g, ln_b, alpha):
    b, t, d = x.shape
    dff = w1.shape[1]
    tm = TOKEN_TILE
    tok = pl.BlockSpec((None, tm, d), lambda bi, i: (bi, i, 0))
    return pl.pallas_call(
        functools.partial(_ffn_kernel, alpha=alpha),
        grid=(b, t // tm),
        in_specs=[tok,
                  pl.BlockSpec((None, N_ADA, d), lambda bi, i: (bi, 0, 0)),
                  _resident(w1.shape), _resident((1, dff)),
                  _resident(w2.shape), _resident((1, d)),
                  _resident((1, d)), _resident((1, d))],
        out_specs=tok,
        out_shape=jax.ShapeDtypeStruct((b, t, d), F32),
        name="ffn",
        compiler_params=pltpu.CompilerParams(
            dimension_semantics=("parallel", "parallel"),
            vmem_limit_bytes=VMEM_LIMIT_BYTES),
    )(x, ada, w1, b1.reshape(1, dff), w2, b2.reshape(1, d),
      ln_g.reshape(1, d), ln_b.reshape(1, d))


def _encoder_layer(x, ada, p, alpha):
    q, k, v, u, vn, sa, sb = _inproj(x, ada, p["w_in"], p["sgu_ln_g"], p["sgu_ln_b"])
    attn = _attention(q, k, v, p["bias"])
    x = _mix(x, attn, u, vn, sa, sb, ada, p["w_s"], p["bs_full"], p["w_attn_up"],
             p["w_sgu_up"], p["w_o"], p["ln1_g"], p["ln1_b"], alpha)
    return _ffn(x, ada, p["w_ff1"], p["b_ff1"], p["w_ff2"], p["b_ff2"],
                p["ln2_g"], p["ln2_b"], alpha)


def kernel(x_prompt, x_sample, c_prompt, c_sample, w_ada, b_ada, w_in, rpb, sgu_ln_g, sgu_ln_b, w_s, b_s, w_attn_up, w_sgu_up, w_o, ln1_g, ln1_b, w_ff1, b_ff1, w_ff2, b_ff2, ln2_g, ln2_b):
    depth, d = w_ada.shape[0], w_ada.shape[1]
    alpha = float((2.0 * depth) ** 0.25)
    nbp = c_prompt.shape[0]
    c_all = jnp.concatenate([c_prompt, c_sample], axis=0)
    xs = [x_prompt, x_sample]
    for l in range(depth):
        p = {
            "w_in": w_in[l].astype(BF16),
            "sgu_ln_g": sgu_ln_g[l], "sgu_ln_b": sgu_ln_b[l],
            "bias": _attn_bias_table(rpb[l]),
            "w_s": jnp.transpose(
                w_s[l].astype(BF16).reshape(SGU_GROUPS // 2, 2, CHUNK, CHUNK),
                (0, 2, 1, 3)).reshape(SGU_GROUPS // 2, CHUNK, 2 * CHUNK),
            "bs_full": jnp.repeat(jnp.transpose(b_s[l]), SGU_GROUP_DIM, axis=1),
            "w_attn_up": w_attn_up[l].astype(BF16),
            "w_sgu_up": w_sgu_up[l].astype(BF16),
            "w_o": w_o[l].astype(BF16),
            "ln1_g": ln1_g[l], "ln1_b": ln1_b[l],
            "w_ff1": w_ff1[l].astype(BF16), "b_ff1": b_ff1[l],
            "w_ff2": w_ff2[l].astype(BF16), "b_ff2": b_ff2[l],
            "ln2_g": ln2_g[l], "ln2_b": ln2_b[l],
        }
        ada = _ada(c_all, w_ada[l], b_ada[l]).reshape(-1, N_ADA, d)
        xs = [_encoder_layer(xs[0], ada[:nbp], p, alpha),
              _encoder_layer(xs[1], ada[nbp:], p, alpha)]
    return (xs[0], xs[1])
```

```python
import functools

import numpy as np
import jax
import jax.numpy as jnp
from jax import lax
from jax.experimental import pallas as pl
from jax.experimental.pallas import tpu as pltpu

GRID_W = 64
NA_HEADS = 8
NA_HEAD_DIM = 64
NA_WIDTH = NA_HEADS * NA_HEAD_DIM
WIN_ROWS = 8
WIN_COLS = 16
SGU_GROUPS = 8
SGU_GROUP_DIM = 64
SGU_WIDTH = SGU_GROUPS * SGU_GROUP_DIM
CHUNK = 128
N_ADA = 6
LN_EPS = 1e-5
MASK_VALUE = -1e30

MXU_DIM = 256
LANES = 128
VMEM_LIMIT_BYTES = 48 * 1024 * 1024

TOKEN_TILE = 1024
SUB_TILE = 2 * CHUNK
ATTN_ROWS = 16
LOG2_E = float(np.log2(np.e))
FF_CHUNK = 1024

BF16 = jnp.bfloat16
F32 = jnp.float32
_SQRT_2_OVER_PI = np.float32(np.sqrt(2.0 / np.pi))


def _gelu_tanh(x):
    return x * (0.5 * (1.0 + jnp.tanh(_SQRT_2_OVER_PI * (x + 0.044715 * (x * x * x)))))


def _layer_norm(x, g, b):
    mu = jnp.mean(x, axis=-1, keepdims=True)
    xc = x - mu
    var = jnp.mean(xc * xc, axis=-1, keepdims=True)
    return xc * lax.rsqrt(var + LN_EPS) * g + b


def _resident(shape):
    zeros = (0,) * len(shape)
    return pl.BlockSpec(shape, lambda *_: zeros, pipeline_mode=pl.Buffered(1))


def _ada_kernel(c_ref, w_ref, b_ref, o_ref):
    c = c_ref[...]
    s = c * jax.nn.sigmoid(c)
    o_ref[...] = jnp.dot(s, w_ref[...], preferred_element_type=F32,
                         precision=lax.Precision.HIGHEST) + b_ref[...]


def _ada(c, w_ada, b_ada):
    nb, d = c.shape
    n = w_ada.shape[1]
    return pl.pallas_call(
        _ada_kernel,
        grid=(n // d,),
        in_specs=[pl.BlockSpec((nb, d), lambda j: (0, 0)),
                  pl.BlockSpec((d, d), lambda j: (0, j)),
                  pl.BlockSpec((1, d), lambda j: (0, j))],
        out_specs=pl.BlockSpec((nb, d), lambda j: (0, j)),
        out_shape=jax.ShapeDtypeStruct((nb, n), F32),
        name="ada",
        compiler_params=pltpu.CompilerParams(vmem_limit_bytes=VMEM_LIMIT_BYTES),
    )(c, w_ada, b_ada.reshape(1, n))


def _inproj_kernel(x_ref, ada_ref, w_ref, lng_ref, lnb_ref,
                   q_ref, k_ref, v_ref, u_ref, vn_ref):
    shift = ada_ref[0:1, :]
    scale = ada_ref[1:2, :]
    u_lo = 3 * NA_WIDTH
    for t in range(x_ref.shape[0] // SUB_TILE):
        rws = slice(t * SUB_TILE, (t + 1) * SUB_TILE)
        h = (x_ref[rws, :] * (1.0 + scale) + shift).astype(BF16)

        def seg(lo, width):
            return jnp.dot(h, w_ref[:, lo:lo + width], preferred_element_type=F32)

        vs = _gelu_tanh(seg(u_lo + SGU_WIDTH, SGU_WIDTH))
        vn_ref[rws, :] = _layer_norm(vs, lng_ref[...], lnb_ref[...]).astype(BF16)
        u_ref[rws, :] = _gelu_tanh(seg(u_lo, SGU_WIDTH)).astype(BF16)
        q_ref[rws, :] = (seg(0, NA_WIDTH) * (LOG2_E * NA_HEAD_DIM ** -0.5)).astype(BF16)
        k_ref[rws, :] = seg(NA_WIDTH, NA_WIDTH).astype(BF16)
        v_ref[rws, :] = seg(2 * NA_WIDTH, NA_WIDTH).astype(BF16)


def _inproj(x, ada, w_in, ln_g, ln_b):
    b, t, d = x.shape
    tm = TOKEN_TILE
    tok = lambda w: pl.BlockSpec((None, tm, w), lambda bi, i: (bi, i, 0))
    widths = (NA_WIDTH, NA_WIDTH, NA_WIDTH, SGU_WIDTH, SGU_WIDTH)
    assert w_in.shape == (d, sum(widths))
    return pl.pallas_call(
        _inproj_kernel,
        grid=(b, t // tm),
        in_specs=[tok(d),
                  pl.BlockSpec((None, N_ADA, d), lambda bi, i: (bi, 0, 0)),
                  _resident(w_in.shape),
                  _resident((1, SGU_WIDTH)),
                  _resident((1, SGU_WIDTH))],
        out_specs=[tok(w) for w in widths],
        out_shape=[jax.ShapeDtypeStruct((b, t, w), BF16) for w in widths],
        name="inproj",
        compiler_params=pltpu.CompilerParams(
            dimension_semantics=("parallel", "parallel"),
            vmem_limit_bytes=VMEM_LIMIT_BYTES),
    )(x, ada, w_in, ln_g.reshape(1, -1), ln_b.reshape(1, -1))


def _attn_window_row0(r0, rows):
    return jnp.clip(r0 - WIN_ROWS // 2, 0, rows - (ATTN_ROWS + WIN_ROWS))


def _attn_kernel(q_ref, k_ref, v_ref, bias_ref, hmask_ref, o_ref, *, rows):
    r0 = pl.program_id(1) * ATTN_ROWS
    win = WIN_ROWS * GRID_W
    win_row0 = _attn_window_row0(r0, rows)
    heads_per_group = MXU_DIM // NA_HEAD_DIM
    heads_per_vreg = LANES // NA_HEAD_DIM
    first_head_lanes = lax.broadcasted_iota(jnp.int32, (GRID_W, LANES), 1) < NA_HEAD_DIM

    for rr in range(ATTN_ROWS):
        r = r0 + rr
        rs = jnp.clip(r - WIN_ROWS // 2, 0, rows - WIN_ROWS)
        delta = r - rs
        koff = pl.multiple_of((rs - win_row0) * GRID_W, GRID_W)
        qrows = slice(rr * GRID_W, (rr + 1) * GRID_W)
        for g in range(NA_WIDTH // MXU_DIM):
            cols = slice(g * MXU_DIM, (g + 1) * MXU_DIM)
            qg = q_ref[qrows, cols]
            kg = k_ref[pl.ds(koff, win), cols]
            vg = v_ref[pl.ds(koff, win), cols]
            qs = jnp.concatenate([qg] * heads_per_group, axis=0) * hmask_ref[...]
            s = lax.dot_general(qs, kg, (((1,), (1,)), ((), ())),
                                preferred_element_type=F32)
            s = s + bias_ref[delta, g]
            m = jnp.max(s, axis=-1, keepdims=True)
            p = jnp.exp2(s - m)
            l = jnp.sum(p, axis=-1, keepdims=True)
            o = jnp.dot(p.astype(BF16), vg, preferred_element_type=F32) * (1.0 / l)
            for j in range(MXU_DIM // LANES):
                blk = [o[(heads_per_vreg * j + e) * GRID_W:(heads_per_vreg * j + e + 1) * GRID_W,
                         j * LANES:(j + 1) * LANES] for e in range(heads_per_vreg)]
                o_ref[qrows, g * MXU_DIM + j * LANES:g * MXU_DIM + (j + 1) * LANES] = (
                    jnp.where(first_head_lanes, blk[0], blk[1]).astype(BF16))


def _attention(q, k, v, bias):
    b, t, w = q.shape
    rows = t // GRID_W
    assert rows % ATTN_ROWS == 0 and rows >= ATTN_ROWS + WIN_ROWS
    cur = pl.BlockSpec((None, ATTN_ROWS * GRID_W, w), lambda bi, i: (bi, i, 0))
    kv = pl.BlockSpec(
        (pl.Element((ATTN_ROWS + WIN_ROWS) * GRID_W), pl.Element(w)),
        lambda bi, i: (pl.multiple_of(
            bi * t + _attn_window_row0(i * ATTN_ROWS, rows) * GRID_W, GRID_W), 0))
    head_of = np.arange(MXU_DIM) // NA_HEAD_DIM
    hmask = jnp.asarray(head_of[:, None] == head_of[None, :], dtype=BF16)
    return pl.pallas_call(
        functools.partial(_attn_kernel, rows=rows),
        grid=(b, rows // ATTN_ROWS),
        in_specs=[cur, kv, kv, _resident(bias.shape), _resident(hmask.shape)],
        out_specs=cur,
        out_shape=jax.ShapeDtypeStruct((b, t, w), BF16),
        name="nattn",
        compiler_params=pltpu.CompilerParams(
            dimension_semantics=("parallel", "parallel"),
            vmem_limit_bytes=VMEM_LIMIT_BYTES),
    )(q, k.reshape(b * t, w), v.reshape(b * t, w), bias, hmask)


def _attn_bias_table(rpb):
    c = np.arange(GRID_W)
    kc = np.arange(GRID_W)
    cs = np.clip(c - WIN_COLS // 2, 0, GRID_W - WIN_COLS)
    valid = (kc[None, :] >= cs[:, None]) & (kc[None, :] < cs[:, None] + WIN_COLS)
    pad = GRID_W - WIN_COLS
    padded = jnp.pad(LOG2_E * rpb.astype(F32), ((0, 0), (0, 0), (pad, pad)))
    planes = jnp.stack([padded[:, :, GRID_W - 1 - ci:2 * GRID_W - 1 - ci] for ci in range(GRID_W)],
                       axis=2)
    planes = jnp.where(valid, planes, MASK_VALUE)
    planes = jnp.transpose(planes, (0, 2, 1, 3))
    t = jnp.stack([planes[:, :, WIN_ROWS - 1 - dl:2 * WIN_ROWS - 1 - dl]
                   for dl in range(WIN_ROWS)])
    return t.reshape(WIN_ROWS, NA_WIDTH // MXU_DIM, MXU_DIM, WIN_ROWS * GRID_W)


def _mix_kernel(x_ref, attn_ref, u_ref, vn_ref, ada_ref, wg_ref,
                ws_ref, bs_ref, wau_ref, wsu_ref, wo_ref, g_ref, b_ref,
                o_ref, *, alpha):
    shift = ada_ref[0:1, :]
    scale = ada_ref[1:2, :]
    gate = ada_ref[2:3, :]
    d = x_ref.shape[1]
    lane = lax.broadcasted_iota(jnp.int32, (CHUNK, LANES), 1)
    lo_lanes = jnp.where(lane < SGU_GROUP_DIM, 1.0, 0.0).astype(BF16)
    hi_lanes = jnp.where(lane < SGU_GROUP_DIM, 0.0, 1.0).astype(BF16)
    for t in range(x_ref.shape[0] // SUB_TILE):
        ra = slice(t * SUB_TILE, t * SUB_TILE + CHUNK)
        rb = slice(t * SUB_TILE + CHUNK, (t + 1) * SUB_TILE)
        rws = slice(t * SUB_TILE, (t + 1) * SUB_TILE)
        s_chunk_a, s_chunk_b = [], []
        for j in range(SGU_WIDTH // LANES):
            cols = slice(j * LANES, (j + 1) * LANES)
            va = vn_ref[ra, cols]
            vb = vn_ref[rb, cols]
            rhs = jnp.concatenate(
                [jnp.concatenate([va * lo_lanes, vb * lo_lanes], axis=1),
                 jnp.concatenate([va * hi_lanes, vb * hi_lanes], axis=1)], axis=0)
            sj = jnp.dot(ws_ref[j], rhs, preferred_element_type=F32)
            s_chunk_a.append(sj[:, :LANES])
            s_chunk_b.append(sj[:, LANES:])
        bs = bs_ref[...]
        s = jnp.concatenate([jnp.concatenate(s_chunk_a, axis=1) + bs,
                             jnp.concatenate(s_chunk_b, axis=1) + bs], axis=0)
        sgu = (u_ref[rws, :].astype(F32) * s).astype(BF16)
        x = x_ref[rws, :]
        h = (x * (1.0 + scale) + shift).astype(BF16)
        ga = jax.nn.sigmoid(jnp.dot(h, wg_ref[:, :d], preferred_element_type=F32))
        a = jnp.dot(attn_ref[rws, :], wau_ref[...], preferred_element_type=F32)
        merged = ga * a
        gb = jax.nn.sigmoid(jnp.dot(h, wg_ref[:, d:], preferred_element_type=F32))
        sg = jnp.dot(sgu, wsu_ref[...], preferred_element_type=F32)
        merged = merged + gb * sg
        mix = jnp.dot(merged.astype(BF16), wo_ref[...], preferred_element_type=F32)
        o_ref[rws, :] = _layer_norm(alpha * x + gate * mix, g_ref[...], b_ref[...])


def _mix(x, attn, u, vn, ada, w_gate, w_s, bs_full, w_au, w_su, w_o, ln_g, ln_b, alpha):
    b, t, d = x.shape
    tm = TOKEN_TILE
    tok = lambda w: pl.BlockSpec((None, tm, w), lambda bi, i: (bi, i, 0))
    return pl.pallas_call(
        functools.partial(_mix_kernel, alpha=alpha),
        grid=(b, t // tm),
        in_specs=[tok(d), tok(NA_WIDTH), tok(SGU_WIDTH), tok(SGU_WIDTH),
                  pl.BlockSpec((None, N_ADA, d), lambda bi, i: (bi, 0, 0)),
                  _resident(w_gate.shape),
                  _resident(w_s.shape), _resident(bs_full.shape),
                  _resident(w_au.shape), _resident(w_su.shape), _resident(w_o.shape),
                  _resident((1, d)), _resident((1, d))],
        out_specs=tok(d),
        out_shape=jax.ShapeDtypeStruct((b, t, d), F32),
        name="mix",
        compiler_params=pltpu.CompilerParams(
            dimension_semantics=("parallel", "parallel"),
            vmem_limit_bytes=VMEM_LIMIT_BYTES),
    )(x, attn, u, vn, ada, w_gate, w_s, bs_full, w_au, w_su, w_o,
      ln_g.reshape(1, d), ln_b.reshape(1, d))


def _ffn_kernel(x_ref, ada_ref, w1_ref, b1_ref, w2_ref, b2_ref, g_ref, b_ref, o_ref,
                *, alpha):
    shift = ada_ref[3:4, :]
    scale = ada_ref[4:5, :]
    gate = ada_ref[5:6, :]
    for s in range(x_ref.shape[0] // SUB_TILE):
        rws = slice(s * SUB_TILE, (s + 1) * SUB_TILE)
        x = x_ref[rws, :]
        h = (x * (1.0 + scale) + shift).astype(BF16)
        f = jnp.zeros(x.shape, F32)
        for c in range(w1_ref.shape[1] // FF_CHUNK):
            cols = slice(c * FF_CHUNK, (c + 1) * FF_CHUNK)
            t = jnp.dot(h, w1_ref[:, cols], preferred_element_type=F32) + b1_ref[:, cols]
            t = jnp.maximum(t, 0.0)
            f = f + jnp.dot((t * t).astype(BF16), w2_ref[cols, :],
                            preferred_element_type=F32)
        f = f + b2_ref[...]
        o_ref[rws, :] = _layer_norm(alpha * x + gate * f, g_ref[...], b_ref[...])


def _ffn(x, ada, w1, b1, w2, b2, ln_g, ln_b, alpha):
    b, t, d = x.shape
    dff = w1.shape[1]
    tm = TOKEN_TILE
    tok = pl.BlockSpec((None, tm, d), lambda bi, i: (bi, i, 0))
    return pl.pallas_call(
        functools.partial(_ffn_kernel, alpha=alpha),
        grid=(b, t // tm),
        in_specs=[tok,
                  pl.BlockSpec((None, N_ADA, d), lambda bi, i: (bi, 0, 0)),
                  _resident(w1.shape), _resident((1, dff)),
                  _resident(w2.shape), _resident((1, d)),
                  _resident((1, d)), _resident((1, d))],
        out_specs=tok,
        out_shape=jax.ShapeDtypeStruct((b, t, d), F32),
        name="ffn",
        compiler_params=pltpu.CompilerParams(
            dimension_semantics=("parallel", "parallel"),
            vmem_limit_bytes=VMEM_LIMIT_BYTES),
    )(x, ada, w1, b1.reshape(1, dff), w2, b2.reshape(1, d),
      ln_g.reshape(1, d), ln_b.reshape(1, d))


def _encoder_layer(x, ada, p, alpha):
    q, k, v, u, vn = _inproj(x, ada, p["w_in"], p["sgu_ln_g"], p["sgu_ln_b"])
    attn = _attention(q, k, v, p["bias"])
    x = _mix(x, attn, u, vn, ada, p["w_gate"], p["w_s"], p["bs_full"], p["w_attn_up"],
             p["w_sgu_up"], p["w_o"], p["ln1_g"], p["ln1_b"], alpha)
    return _ffn(x, ada, p["w_ff1"], p["b_ff1"], p["w_ff2"], p["b_ff2"],
                p["ln2_g"], p["ln2_b"], alpha)


def kernel(x_prompt, x_sample, c_prompt, c_sample, w_ada, b_ada, w_in, rpb, sgu_ln_g, sgu_ln_b, w_s, b_s, w_attn_up, w_sgu_up, w_o, ln1_g, ln1_b, w_ff1, b_ff1, w_ff2, b_ff2, ln2_g, ln2_b):
    depth, d = w_ada.shape[0], w_ada.shape[1]
    alpha = float((2.0 * depth) ** 0.25)
    nbp = c_prompt.shape[0]
    c_all = jnp.concatenate([c_prompt, c_sample], axis=0)
    xs = [x_prompt, x_sample]
    for l in range(depth):
        p = {
            "w_in": w_in[l, :, :3 * NA_WIDTH + 2 * SGU_WIDTH].astype(BF16),
            "w_gate": w_in[l, :, 3 * NA_WIDTH + 2 * SGU_WIDTH:].astype(BF16),
            "sgu_ln_g": sgu_ln_g[l], "sgu_ln_b": sgu_ln_b[l],
            "bias": _attn_bias_table(rpb[l]),
            "w_s": jnp.transpose(
                w_s[l].astype(BF16).reshape(SGU_GROUPS // 2, 2, CHUNK, CHUNK),
                (0, 2, 1, 3)).reshape(SGU_GROUPS // 2, CHUNK, 2 * CHUNK),
            "bs_full": jnp.repeat(jnp.transpose(b_s[l]), SGU_GROUP_DIM, axis=1),
            "w_attn_up": w_attn_up[l].astype(BF16),
            "w_sgu_up": w_sgu_up[l].astype(BF16),
            "w_o": w_o[l].astype(BF16),
            "ln1_g": ln1_g[l], "ln1_b": ln1_b[l],
            "w_ff1": w_ff1[l].astype(BF16), "b_ff1": b_ff1[l],
            "w_ff2": w_ff2[l].astype(BF16), "b_ff2": b_ff2[l],
            "ln2_g": ln2_g[l], "ln2_b": ln2_b[l],
        }
        ada = _ada(c_all, w_ada[l], b_ada[l]).reshape(-1, N_ADA, d)
        xs = [_encoder_layer(xs[0], ada[:nbp], p, alpha),
              _encoder_layer(xs[1], ada[nbp:], p, alpha)]
    return (xs[0], xs[1])
```

```python
import functools

import numpy as np
import jax
import jax.numpy as jnp
from jax import lax
from jax.experimental import pallas as pl
from jax.experimental.pallas import tpu as pltpu

GRID_W = 64
NA_HEADS = 8
NA_HEAD_DIM = 64
NA_WIDTH = NA_HEADS * NA_HEAD_DIM
WIN_ROWS = 8
WIN_COLS = 16
SGU_GROUPS = 8
SGU_GROUP_DIM = 64
SGU_WIDTH = SGU_GROUPS * SGU_GROUP_DIM
CHUNK = 128
N_ADA = 6
LN_EPS = 1e-5
MASK_VALUE = -1e30

MXU_DIM = 256
LANES = 128
VMEM_LIMIT_BYTES = 48 * 1024 * 1024

TOKEN_TILE = 1024
SUB_TILE = 2 * CHUNK
ATTN_ROWS = 16
LOG2_E = float(np.log2(np.e))
FF_CHUNK = 1024

BF16 = jnp.bfloat16
F32 = jnp.float32
_SQRT_2_OVER_PI = np.float32(np.sqrt(2.0 / np.pi))


def _gelu_tanh(x):
    return x * (0.5 * (1.0 + jnp.tanh(_SQRT_2_OVER_PI * (x + 0.044715 * (x * x * x)))))


def _layer_norm(x, g, b):
    mu = jnp.mean(x, axis=-1, keepdims=True)
    xc = x - mu
    var = jnp.mean(xc * xc, axis=-1, keepdims=True)
    return xc * lax.rsqrt(var + LN_EPS) * g + b


def _token_grid_params():
    return pltpu.CompilerParams(
        dimension_semantics=("parallel", "parallel"),
        vmem_limit_bytes=VMEM_LIMIT_BYTES)


def _resident(shape):
    zeros = (0,) * len(shape)
    return pl.BlockSpec(shape, lambda *_: zeros, pipeline_mode=pl.Buffered(1))


def _ada_kernel(c_ref, w_ref, b_ref, o_ref):
    c = c_ref[...]
    s = c * jax.nn.sigmoid(c)
    o_ref[...] = jnp.dot(s, w_ref[...], preferred_element_type=F32,
                         precision=lax.Precision.HIGHEST) + b_ref[...]


def _ada(c, w_ada, b_ada):
    nb, d = c.shape
    n = w_ada.shape[1]
    return pl.pallas_call(
        _ada_kernel,
        grid=(n // d,),
        in_specs=[pl.BlockSpec((nb, d), lambda j: (0, 0)),
                  pl.BlockSpec((d, d), lambda j: (0, j)),
                  pl.BlockSpec((1, d), lambda j: (0, j))],
        out_specs=pl.BlockSpec((nb, d), lambda j: (0, j)),
        out_shape=jax.ShapeDtypeStruct((nb, n), F32),
        name="ada",
        compiler_params=pltpu.CompilerParams(vmem_limit_bytes=VMEM_LIMIT_BYTES),
    )(c, w_ada, b_ada.reshape(1, n))


def _inproj_kernel(x_ref, ada_ref, w_ref, lng_ref, lnb_ref,
                   q_ref, k_ref, v_ref, u_ref, vn_ref):
    shift = ada_ref[0:1, :]
    scale = ada_ref[1:2, :]
    u_lo = 3 * NA_WIDTH
    for t in range(x_ref.shape[0] // SUB_TILE):
        rws = slice(t * SUB_TILE, (t + 1) * SUB_TILE)
        h = (x_ref[rws, :] * (1.0 + scale) + shift).astype(BF16)

        def seg(lo, width):
            return jnp.dot(h, w_ref[:, lo:lo + width], preferred_element_type=F32)

        vs = _gelu_tanh(seg(u_lo + SGU_WIDTH, SGU_WIDTH))
        vn_ref[rws, :] = _layer_norm(vs, lng_ref[...], lnb_ref[...]).astype(BF16)
        u_ref[rws, :] = _gelu_tanh(seg(u_lo, SGU_WIDTH)).astype(BF16)
        q_ref[rws, :] = (seg(0, NA_WIDTH) * (LOG2_E * NA_HEAD_DIM ** -0.5)).astype(BF16)
        k_ref[rws, :] = seg(NA_WIDTH, NA_WIDTH).astype(BF16)
        v_ref[rws, :] = seg(2 * NA_WIDTH, NA_WIDTH).astype(BF16)


def _inproj(x, ada, w_in, ln_g, ln_b):
    b, t, d = x.shape
    tm = TOKEN_TILE
    tok = lambda w: pl.BlockSpec((None, tm, w), lambda bi, i: (bi, i, 0))
    widths = (NA_WIDTH, NA_WIDTH, NA_WIDTH, SGU_WIDTH, SGU_WIDTH)
    assert w_in.shape == (d, sum(widths))
    return pl.pallas_call(
        _inproj_kernel,
        grid=(b, t // tm),
        in_specs=[tok(d),
                  pl.BlockSpec((None, N_ADA, d), lambda bi, i: (bi, 0, 0)),
                  _resident(w_in.shape),
                  _resident((1, SGU_WIDTH)),
                  _resident((1, SGU_WIDTH))],
        out_specs=[tok(w) for w in widths],
        out_shape=[jax.ShapeDtypeStruct((b, t, w), BF16) for w in widths],
        name="inproj",
        compiler_params=_token_grid_params(),
    )(x, ada, w_in, ln_g.reshape(1, -1), ln_b.reshape(1, -1))


def _attn_window_row0(r0, rows):
    return jnp.clip(r0 - WIN_ROWS // 2, 0, rows - (ATTN_ROWS + WIN_ROWS))


def _attn_kernel(q_ref, k_ref, v_ref, bias_ref, hmask_ref, o_ref, *, rows):
    r0 = pl.program_id(1) * ATTN_ROWS
    win = WIN_ROWS * GRID_W
    win_row0 = _attn_window_row0(r0, rows)
    heads_per_group = MXU_DIM // NA_HEAD_DIM
    heads_per_vreg = LANES // NA_HEAD_DIM
    first_head_lanes = lax.broadcasted_iota(jnp.int32, (GRID_W, LANES), 1) < NA_HEAD_DIM

    for rr in range(ATTN_ROWS):
        r = r0 + rr
        rs = jnp.clip(r - WIN_ROWS // 2, 0, rows - WIN_ROWS)
        delta = r - rs
        koff = pl.multiple_of((rs - win_row0) * GRID_W, GRID_W)
        qrows = slice(rr * GRID_W, (rr + 1) * GRID_W)
        for g in range(NA_WIDTH // MXU_DIM):
            cols = slice(g * MXU_DIM, (g + 1) * MXU_DIM)
            qg = q_ref[qrows, cols]
            kg = k_ref[pl.ds(koff, win), cols]
            vg = v_ref[pl.ds(koff, win), cols]
            qs = jnp.concatenate([qg] * heads_per_group, axis=0) * hmask_ref[...]
            s = lax.dot_general(qs, kg, (((1,), (1,)), ((), ())),
                                preferred_element_type=F32)
            s = s + bias_ref[delta, g]
            m = jnp.max(s, axis=-1, keepdims=True)
            p = jnp.exp2(s - m)
            l = jnp.sum(p, axis=-1, keepdims=True)
            o = jnp.dot(p.astype(BF16), vg, preferred_element_type=F32) * (1.0 / l)
            for j in range(MXU_DIM // LANES):
                blk = [o[(heads_per_vreg * j + e) * GRID_W:(heads_per_vreg * j + e + 1) * GRID_W,
                         j * LANES:(j + 1) * LANES] for e in range(heads_per_vreg)]
                o_ref[qrows, g * MXU_DIM + j * LANES:g * MXU_DIM + (j + 1) * LANES] = (
                    jnp.where(first_head_lanes, blk[0], blk[1]).astype(BF16))


def _attention(q, k, v, bias):
    b, t, w = q.shape
    rows = t // GRID_W
    assert rows % ATTN_ROWS == 0 and rows >= ATTN_ROWS + WIN_ROWS
    cur = pl.BlockSpec((None, ATTN_ROWS * GRID_W, w), lambda bi, i: (bi, i, 0))
    kv = pl.BlockSpec(
        (pl.Element((ATTN_ROWS + WIN_ROWS) * GRID_W), pl.Element(w)),
        lambda bi, i: (pl.multiple_of(
            bi * t + _attn_window_row0(i * ATTN_ROWS, rows) * GRID_W, GRID_W), 0))
    head_of = np.arange(MXU_DIM) // NA_HEAD_DIM
    hmask = jnp.asarray(head_of[:, None] == head_of[None, :], dtype=BF16)
    return pl.pallas_call(
        functools.partial(_attn_kernel, rows=rows),
        grid=(b, rows // ATTN_ROWS),
        in_specs=[cur, kv, kv, _resident(bias.shape), _resident(hmask.shape)],
        out_specs=cur,
        out_shape=jax.ShapeDtypeStruct((b, t, w), BF16),
        name="nattn",
        compiler_params=_token_grid_params(),
    )(q, k.reshape(b * t, w), v.reshape(b * t, w), bias, hmask)


def _attn_bias_table(rpb):
    c = np.arange(GRID_W)
    kc = np.arange(GRID_W)
    cs = np.clip(c - WIN_COLS // 2, 0, GRID_W - WIN_COLS)
    valid = (kc[None, :] >= cs[:, None]) & (kc[None, :] < cs[:, None] + WIN_COLS)
    ncol = 2 * WIN_COLS - 1
    nrow = 2 * WIN_ROWS - 1
    dcol = kc[None, :] - c[:, None] + (WIN_COLS - 1)
    onehot = ((dcol[None] == np.arange(ncol)[:, None, None]) & valid[None]).astype(np.float32)
    planes = jnp.dot(LOG2_E * rpb.astype(F32).reshape(NA_HEADS * nrow, ncol),
                     onehot.reshape(ncol, GRID_W * GRID_W), precision=lax.Precision.HIGHEST)
    planes = jnp.where(valid.reshape(1, GRID_W * GRID_W), planes, MASK_VALUE)
    planes = jnp.transpose(planes.reshape(NA_HEADS, nrow, GRID_W, GRID_W), (0, 2, 1, 3))
    planes = planes.reshape(NA_HEADS, GRID_W, nrow * GRID_W)
    t = jnp.stack([planes[:, :, (WIN_ROWS - 1 - dl) * GRID_W:(2 * WIN_ROWS - 1 - dl) * GRID_W]
                   for dl in range(WIN_ROWS)])
    return t.reshape(WIN_ROWS, NA_WIDTH // MXU_DIM, MXU_DIM, WIN_ROWS * GRID_W)


def _mix_kernel(x_ref, attn_ref, u_ref, vn_ref, ada_ref, wg_ref,
                ws_ref, bs_ref, wau_ref, wsu_ref, wo_ref, g_ref, b_ref,
                o_ref, *, alpha):
    shift = ada_ref[0:1, :]
    scale = ada_ref[1:2, :]
    gate = ada_ref[2:3, :]
    d = x_ref.shape[1]
    lane = lax.broadcasted_iota(jnp.int32, (CHUNK, LANES), 1)
    lo_lanes = jnp.where(lane < SGU_GROUP_DIM, 1.0, 0.0).astype(BF16)
    hi_lanes = jnp.where(lane < SGU_GROUP_DIM, 0.0, 1.0).astype(BF16)
    for t in range(x_ref.shape[0] // SUB_TILE):
        ra = slice(t * SUB_TILE, t * SUB_TILE + CHUNK)
        rb = slice(t * SUB_TILE + CHUNK, (t + 1) * SUB_TILE)
        rws = slice(t * SUB_TILE, (t + 1) * SUB_TILE)
        s_chunk_a, s_chunk_b = [], []
        for j in range(SGU_WIDTH // LANES):
            cols = slice(j * LANES, (j + 1) * LANES)
            va = vn_ref[ra, cols]
            vb = vn_ref[rb, cols]
            rhs = jnp.concatenate(
                [jnp.concatenate([va * lo_lanes, vb * lo_lanes], axis=1),
                 jnp.concatenate([va * hi_lanes, vb * hi_lanes], axis=1)], axis=0)
            sj = jnp.dot(ws_ref[j], rhs, preferred_element_type=F32)
            s_chunk_a.append(sj[:, :LANES])
            s_chunk_b.append(sj[:, LANES:])
        bs = bs_ref[...]
        s = jnp.concatenate([jnp.concatenate(s_chunk_a, axis=1) + bs,
                             jnp.concatenate(s_chunk_b, axis=1) + bs], axis=0)
        sgu = (u_ref[rws, :].astype(F32) * s).astype(BF16)
        x = x_ref[rws, :]
        h = (x * (1.0 + scale) + shift).astype(BF16)
        ga = jax.nn.sigmoid(jnp.dot(h, wg_ref[:, :d], preferred_element_type=F32))
        a = jnp.dot(attn_ref[rws, :], wau_ref[...], preferred_element_type=F32)
        merged = ga * a
        gb = jax.nn.sigmoid(jnp.dot(h, wg_ref[:, d:], preferred_element_type=F32))
        sg = jnp.dot(sgu, wsu_ref[...], preferred_element_type=F32)
        merged = merged + gb * sg
        mix = jnp.dot(merged.astype(BF16), wo_ref[...], preferred_element_type=F32)
        o_ref[rws, :] = _layer_norm(alpha * x + gate * mix, g_ref[...], b_ref[...])


def _mix(x, attn, u, vn, ada, w_gate, w_s, bs_full, w_au, w_su, w_o, ln_g, ln_b, alpha):
    b, t, d = x.shape
    tm = TOKEN_TILE
    tok = lambda w: pl.BlockSpec((None, tm, w), lambda bi, i: (bi, i, 0))
    return pl.pallas_call(
        functools.partial(_mix_kernel, alpha=alpha),
        grid=(b, t // tm),
        in_specs=[tok(d), tok(NA_WIDTH), tok(SGU_WIDTH), tok(SGU_WIDTH),
                  pl.BlockSpec((None, N_ADA, d), lambda bi, i: (bi, 0, 0)),
                  _resident(w_gate.shape),
                  _resident(w_s.shape), _resident(bs_full.shape),
                  _resident(w_au.shape), _resident(w_su.shape), _resident(w_o.shape),
                  _resident((1, d)), _resident((1, d))],
        out_specs=tok(d),
        out_shape=jax.ShapeDtypeStruct((b, t, d), F32),
        name="mix",
        compiler_params=_token_grid_params(),
    )(x, attn, u, vn, ada, w_gate, w_s, bs_full, w_au, w_su, w_o,
      ln_g.reshape(1, d), ln_b.reshape(1, d))


def _ffn_kernel(x_ref, ada_ref, w1_ref, b1_ref, w2_ref, b2_ref, g_ref, b_ref, o_ref,
                *, alpha):
    shift = ada_ref[3:4, :]
    scale = ada_ref[4:5, :]
    gate = ada_ref[5:6, :]
    for s in range(x_ref.shape[0] // SUB_TILE):
        rws = slice(s * SUB_TILE, (s + 1) * SUB_TILE)
        x = x_ref[rws, :]
        h = (x * (1.0 + scale) + shift).astype(BF16)
        f = jnp.zeros(x.shape, F32)
        for c in range(w1_ref.shape[1] // FF_CHUNK):
            cols = slice(c * FF_CHUNK, (c + 1) * FF_CHUNK)
            t = jnp.dot(h, w1_ref[:, cols], preferred_element_type=F32) + b1_ref[:, cols]
            t = jnp.maximum(t, 0.0)
            f = f + jnp.dot((t * t).astype(BF16), w2_ref[cols, :],
                            preferred_element_type=F32)
        f = f + b2_ref[...]
        o_ref[rws, :] = _layer_norm(alpha * x + gate * f, g_ref[...], b_ref[...])


def _ffn(x, ada, w1, b1, w2, b2, ln_g, ln_b, alpha):
    b, t, d = x.shape
    dff = w1.shape[1]
    tm = TOKEN_TILE
    tok = pl.BlockSpec((None, tm, d), lambda bi, i: (bi, i, 0))
    return pl.pallas_call(
        functools.partial(_ffn_kernel, alpha=alpha),
        grid=(b, t // tm),
        in_specs=[tok,
                  pl.BlockSpec((None, N_ADA, d), lambda bi, i: (bi, 0, 0)),
                  _resident(w1.shape), _resident((1, dff)),
                  _resident(w2.shape), _resident((1, d)),
                  _resident((1, d)), _resident((1, d))],
        out_specs=tok,
        out_shape=jax.ShapeDtypeStruct((b, t, d), F32),
        name="ffn",
        compiler_params=_token_grid_params(),
    )(x, ada, w1, b1.reshape(1, dff), w2, b2.reshape(1, d),
      ln_g.reshape(1, d), ln_b.reshape(1, d))


def _encoder_layer(x, ada, p, alpha):
    q, k, v, u, vn = _inproj(x, ada, p["w_in"], p["sgu_ln_g"], p["sgu_ln_b"])
    attn = _attention(q, k, v, p["bias"])
    x = _mix(x, attn, u, vn, ada, p["w_gate"], p["w_s"], p["bs_full"], p["w_attn_up"],
             p["w_sgu_up"], p["w_o"], p["ln1_g"], p["ln1_b"], alpha)
    return _ffn(x, ada, p["w_ff1"], p["b_ff1"], p["w_ff2"], p["b_ff2"],
                p["ln2_g"], p["ln2_b"], alpha)


def kernel(x_prompt, x_sample, c_prompt, c_sample, w_ada, b_ada, w_in, rpb, sgu_ln_g, sgu_ln_b, w_s, b_s, w_attn_up, w_sgu_up, w_o, ln1_g, ln1_b, w_ff1, b_ff1, w_ff2, b_ff2, ln2_g, ln2_b):
    depth, d = w_ada.shape[0], w_ada.shape[1]
    alpha = float((2.0 * depth) ** 0.25)
    nbp = c_prompt.shape[0]
    c_all = jnp.concatenate([c_prompt, c_sample], axis=0)
    xs = [x_prompt, x_sample]
    for l in range(depth):
        p = {
            "w_in": w_in[l, :, :3 * NA_WIDTH + 2 * SGU_WIDTH].astype(BF16),
            "w_gate": w_in[l, :, 3 * NA_WIDTH + 2 * SGU_WIDTH:].astype(BF16),
            "sgu_ln_g": sgu_ln_g[l], "sgu_ln_b": sgu_ln_b[l],
            "bias": _attn_bias_table(rpb[l]),
            "w_s": jnp.transpose(
                w_s[l].astype(BF16).reshape(SGU_GROUPS // 2, 2, CHUNK, CHUNK),
                (0, 2, 1, 3)).reshape(SGU_GROUPS // 2, CHUNK, 2 * CHUNK),
            "bs_full": jnp.repeat(jnp.transpose(b_s[l]), SGU_GROUP_DIM, axis=1),
            "w_attn_up": w_attn_up[l].astype(BF16),
            "w_sgu_up": w_sgu_up[l].astype(BF16),
            "w_o": w_o[l].astype(BF16),
            "ln1_g": ln1_g[l], "ln1_b": ln1_b[l],
            "w_ff1": w_ff1[l].astype(BF16), "b_ff1": b_ff1[l],
            "w_ff2": w_ff2[l].astype(BF16), "b_ff2": b_ff2[l],
            "ln2_g": ln2_g[l], "ln2_b": ln2_b[l],
        }
        ada = _ada(c_all, w_ada[l], b_ada[l]).reshape(-1, N_ADA, d)
        xs = [_encoder_layer(xs[0], ada[:nbp], p, alpha),
              _encoder_layer(xs[1], ada[nbp:], p, alpha)]
    return (xs[0], xs[1])
```
